```python
import math
import jax, jax.numpy as jnp
from jax import lax
import numpy as np

D_MODEL = 1024
BATCH = 16
SEQ = 2048
DEPTH = 2

GRID_W = 64
CTX_LEN = 256
D_BRANCH = D_MODEL // 2
HEAD_DIM = 64
CONV_K = 31
DIFF_D = 64
DIFF_HEADS = D_BRANCH // (2 * DIFF_D)
NAT_HEADS = D_BRANCH // HEAD_DIM
NAT_KR = 8
NAT_KC = 16
N_EXPERTS = 32
TOP_K = 4
D_FF = D_MODEL
SWIGLU_LIMIT = 7.0
SWIGLU_ALPHA = 1.702
ROPE_BASE = 10000.0
Q_BLOCK = 128
LN_EPS = 1e-6
DEEPNORM_ALPHA = (2 * DEPTH) ** 0.25
DEEPNORM_BETA = (8 * DEPTH) ** -0.25
IN_COLS = 2 * D_BRANCH + 3 * D_BRANCH + 3 * D_BRANCH + 3 * D_MODEL

kernel_name = 'hybrid_gated_conv_diffattn_natten_moe_dit'


def layer_norm(x, g=None, b=None):
    xf = x.astype(jnp.float32)
    mu = jnp.mean(xf, axis=-1, keepdims=True)
    var = jnp.mean(jnp.square(xf - mu), axis=-1, keepdims=True)
    y = (xf - mu) * lax.rsqrt(var + LN_EPS)
    if g is not None:
        y = y * g.astype(jnp.float32) + b.astype(jnp.float32)
    return y.astype(x.dtype)


def rms_norm(x, g):
    xf = x.astype(jnp.float32)
    y = xf * lax.rsqrt(jnp.mean(jnp.square(xf), axis=-1, keepdims=True) + LN_EPS)
    return (y * g.astype(jnp.float32)).astype(x.dtype)


def modulate(x, shift, scale):
    return layer_norm(x) * (1.0 + scale) + shift


def axial_rope_angles(n_tokens, dim):
    axis_dim = dim // 2
    inv_freq = ROPE_BASE ** (-jnp.arange(0, axis_dim, 2, dtype=jnp.float32) / axis_dim)
    t = jnp.arange(n_tokens, dtype=jnp.int32)
    row = (t // GRID_W).astype(jnp.float32)
    col = (t % GRID_W).astype(jnp.float32)
    return row[:, None] * inv_freq[None, :], col[:, None] * inv_freq[None, :]


def _rotate(x, ang):
    x1, x2 = jnp.split(x.astype(jnp.float32), 2, axis=-1)
    cos, sin = jnp.cos(ang), jnp.sin(ang)
    return jnp.concatenate([x1 * cos - x2 * sin, x2 * cos + x1 * sin], axis=-1)


def apply_axial_rope(x, ang_row, ang_col):
    h = x.shape[-1] // 2
    return jnp.concatenate([_rotate(x[..., :h], ang_row), _rotate(x[..., h:], ang_col)], axis=-1).astype(x.dtype)


def split_cols(u):
    sizes = [2 * D_BRANCH] + [D_BRANCH] * 6 + [3 * D_MODEL]
    return jnp.split(u, np.cumsum(sizes)[:-1].tolist(), axis=-1)


def to_heads(t, n_heads):
    b, l, _ = t.shape
    return t.reshape(b, l, n_heads, -1).transpose(0, 2, 1, 3)


def from_heads(o):
    b, h, l, d = o.shape
    return o.transpose(0, 2, 1, 3).reshape(b, l, h * d)


def conformer_conv(u, dw_w, dw_b, ln_g, ln_b):
    val, gate = jnp.split(u, 2, axis=-1)
    h = val * jax.nn.sigmoid(gate)
    h = lax.conv_general_dilated(h, dw_w[:, None, :], (1,), [(CONV_K // 2, CONV_K // 2)],
                                 dimension_numbers=('NWC', 'WIO', 'NWC'),
                                 feature_group_count=h.shape[-1]) + dw_b
    return jax.nn.silu(layer_norm(h, ln_g, ln_b))


def diff_attend(q1, q2, k1, k2, v, lam):
    scale = DIFF_D ** -0.5
    s1 = jnp.einsum('bhqd,bhkd->bhqk', q1, k1).astype(jnp.float32) * scale
    s2 = jnp.einsum('bhqd,bhkd->bhqk', q2, k2).astype(jnp.float32) * scale
    a = jax.nn.softmax(s1, axis=-1) - lam * jax.nn.softmax(s2, axis=-1)
    return jnp.einsum('bhqk,bhkd->bhqd', a.astype(v.dtype), v)


def diff_head_out(o, subln_g, lam_init):
    return from_heads(rms_norm(o, subln_g) * (1.0 - lam_init))


def softmax_attend(q, k, v):
    s = jnp.einsum('bhqd,bhkd->bhqk', q, k).astype(jnp.float32) * (q.shape[-1] ** -0.5)
    return jnp.einsum('bhqk,bhkd->bhqd', jax.nn.softmax(s, axis=-1).astype(v.dtype), v)


def neighbourhood_attention(q, k, v, k_ctx, v_ctx, rpb):
    b, h, l, dh = q.shape
    rows = l // GRID_W
    kr = min(NAT_KR, rows)
    kc = NAT_KC
    qg = q.reshape(b, h, rows, GRID_W, dh)
    kg = k.reshape(b, h, rows, GRID_W, dh)
    vg = v.reshape(b, h, rows, GRID_W, dh)
    cols = np.arange(GRID_W)
    col_start = np.clip(cols - kc // 2, 0, GRID_W - kc)
    col_idx = col_start[:, None] + np.arange(kc)[None, :]
    col_off = col_idx - cols[:, None] + (NAT_KC - 1)
    rpb_c = rpb[:, :, col_off]
    scale = dh ** -0.5
    n_loc = kr * kc

    def one_row(r):
        rs = jnp.clip(r - kr // 2, 0, rows - kr)
        ks = lax.dynamic_slice_in_dim(kg, rs, kr, axis=2)[:, :, :, col_idx]
        vs = lax.dynamic_slice_in_dim(vg, rs, kr, axis=2)[:, :, :, col_idx]
        qr = lax.dynamic_index_in_dim(qg, r, axis=2, keepdims=False)
        s_loc = jnp.einsum('bhwd,bhiwjd->bhwij', qr, ks).astype(jnp.float32) * scale
        row_off = rs + jnp.arange(kr) - r + (NAT_KR - 1)
        bias = jnp.take(rpb_c, row_off, axis=1).transpose(0, 2, 1, 3)
        s_loc = s_loc + bias[None].astype(jnp.float32)
        s_ctx = jnp.einsum('bhwd,bhcd->bhwc', qr, k_ctx).astype(jnp.float32) * scale
        p = jax.nn.softmax(jnp.concatenate([s_loc.reshape(b, h, GRID_W, n_loc), s_ctx], axis=-1), axis=-1)
        p_loc = p[..., :n_loc].reshape(b, h, GRID_W, kr, kc).astype(v.dtype)
        p_ctx = p[..., n_loc:].astype(v.dtype)
        return (jnp.einsum('bhwij,bhiwjd->bhwd', p_loc, vs)
                + jnp.einsum('bhwc,bhcd->bhwd', p_ctx, v_ctx))

    out = lax.map(one_row, jnp.arange(rows))
    return out.transpose(1, 2, 0, 3, 4).reshape(b, h, l, dh)


def merge_branches(ya, yb, yc, g, w_branch, w_out):
    ga, gb, gc = jnp.split(jax.nn.sigmoid(g), 3, axis=-1)
    m = ga * (ya @ w_branch[0]) + gb * (yb @ w_branch[1]) + gc * (yc @ w_branch[2])
    return m @ w_out


def mixing_sublayer(h_lat, h_ctx, w_in, dw_w, dw_b, cln_g, cln_b, lam, lam_init, subln_g, rpb,
                    w_branch, w_out, with_ctx):
    b, l, _ = h_lat.shape
    ang_r, ang_c = axial_rope_angles(l, DIFF_D)
    a_l, bq_l, bk_l, bv_l, cq_l, ck_l, cv_l, g_l = split_cols(h_lat @ w_in)
    a_c, bq_c, bk_c, bv_c, cq_c, ck_c, cv_c, g_c = split_cols(h_ctx @ w_in)

    q_l, k_l, v_l = to_heads(bq_l, DIFF_HEADS), to_heads(bk_l, DIFF_HEADS), to_heads(bv_l, DIFF_HEADS)
    q_c, k_c, v_c = to_heads(bq_c, DIFF_HEADS), to_heads(bk_c, DIFF_HEADS), to_heads(bv_c, DIFF_HEADS)
    q1_l = apply_axial_rope(q_l[..., :DIFF_D], ang_r, ang_c)
    q2_l = apply_axial_rope(q_l[..., DIFF_D:], ang_r, ang_c)
    k1_all = jnp.concatenate([k_c[..., :DIFF_D], apply_axial_rope(k_l[..., :DIFF_D], ang_r, ang_c)], axis=2)
    k2_all = jnp.concatenate([k_c[..., DIFF_D:], apply_axial_rope(k_l[..., DIFF_D:], ang_r, ang_c)], axis=2)
    v_all = jnp.concatenate([v_c, v_l], axis=2)
    nb = l // Q_BLOCK

    def blocks(t):
        return t.reshape(b, DIFF_HEADS, nb, Q_BLOCK, DIFF_D).transpose(2, 0, 1, 3, 4)

    ob = lax.map(lambda qq: diff_attend(qq[0], qq[1], k1_all, k2_all, v_all, lam), (blocks(q1_l), blocks(q2_l)))
    ob = ob.transpose(1, 2, 0, 3, 4).reshape(b, DIFF_HEADS, l, 2 * DIFF_D)
    yb_l = diff_head_out(ob, subln_g, lam_init)

    nk_c, nv_c = to_heads(ck_c, NAT_HEADS), to_heads(cv_c, NAT_HEADS)
    yc_l = from_heads(neighbourhood_attention(to_heads(cq_l, NAT_HEADS), to_heads(ck_l, NAT_HEADS),
                                              to_heads(cv_l, NAT_HEADS), nk_c, nv_c, rpb))
    ya_l = conformer_conv(a_l, dw_w, dw_b, cln_g, cln_b)
    y_lat = merge_branches(ya_l, yb_l, yc_l, g_l, w_branch, w_out)
    if not with_ctx:
        return y_lat, None

    ya_c = conformer_conv(a_c, dw_w, dw_b, cln_g, cln_b)
    yb_c = diff_head_out(diff_attend(q_c[..., :DIFF_D], q_c[..., DIFF_D:], k_c[..., :DIFF_D], k_c[..., DIFF_D:],
                                     v_c, lam), subln_g, lam_init)
    yc_c = from_heads(softmax_attend(to_heads(cq_c, NAT_HEADS), nk_c, nv_c))
    y_ctx = merge_branches(ya_c, yb_c, yc_c, g_c, w_branch, w_out)
    return y_lat, y_ctx


def routed_moe(t, w_r, b_r, w1, b1, w2, b2):
    logits = (t @ w_r).astype(jnp.float32) + b_r.astype(jnp.float32)
    top_v, top_i = lax.top_k(logits, TOP_K)
    top_w = jax.nn.softmax(top_v, axis=-1)
    combine = jnp.sum(jax.nn.one_hot(top_i, N_EXPERTS, dtype=jnp.float32) * top_w[..., None], axis=1)
    out = jnp.zeros(t.shape, jnp.float32)
    for e in range(N_EXPERTS):
        hh = t @ w1[e] + b1[e]
        glu = jnp.minimum(hh[:, :D_FF], SWIGLU_LIMIT)
        lin = jnp.clip(hh[:, D_FF:], -SWIGLU_LIMIT, SWIGLU_LIMIT)
        act = glu * jax.nn.sigmoid(SWIGLU_ALPHA * glu) * (lin + 1.0)
        out = out + combine[:, e:e + 1] * (act @ w2[e] + b2[e]).astype(jnp.float32)
    return out.astype(t.dtype)


def setup_inputs(seed: int = 0) -> dict:
    key = jax.random.key(seed)
    ks = jax.random.split(key, 32)
    D = D_MODEL

    def nrm(k, shape, s):
        return jax.random.normal(k, shape, jnp.float32) * s

    return {
        'x': nrm(ks[0], (BATCH, SEQ, D), 1.0),
        'c': nrm(ks[1], (BATCH, D), 1.0),
        'ctx': nrm(ks[2], (BATCH, CTX_LEN, D), 1.0),
        'c_ctx': nrm(ks[3], (D,), 1.0),
        'w_ada': nrm(ks[4], (DEPTH, D, 6 * D), 0.5 * D ** -0.5),
        'b_ada': nrm(ks[5], (DEPTH, 6 * D), 0.02),
        'w_in': nrm(ks[6], (DEPTH, D, IN_COLS), D ** -0.5),
        'conv_dw_w': nrm(ks[7], (DEPTH, CONV_K, D_BRANCH), CONV_K ** -0.5),
        'conv_dw_b': nrm(ks[8], (DEPTH, D_BRANCH), 0.02),
        'conv_ln_g': 1.0 + nrm(ks[9], (DEPTH, D_BRANCH), 0.02),
        'conv_ln_b': nrm(ks[10], (DEPTH, D_BRANCH), 0.02),
        'lam_q1': nrm(ks[11], (DEPTH, DIFF_D), 0.1),
        'lam_k1': nrm(ks[12], (DEPTH, DIFF_D), 0.1),
        'lam_q2': nrm(ks[13], (DEPTH, DIFF_D), 0.1),
        'lam_k2': nrm(ks[14], (DEPTH, DIFF_D), 0.1),
        'diff_subln_g': 1.0 + nrm(ks[15], (DEPTH, 2 * DIFF_D), 0.02),
        'nat_rpb': nrm(ks[16], (DEPTH, NAT_HEADS, 2 * NAT_KR - 1, 2 * NAT_KC - 1), 0.05),
        'w_branch': nrm(ks[17], (DEPTH, 3, D_BRANCH, D), D_BRANCH ** -0.5),
        'w_out': nrm(ks[18], (DEPTH, D, D), DEEPNORM_BETA * D ** -0.5),
        'ln1_g': 1.0 + nrm(ks[19], (DEPTH, D), 0.02),
        'ln1_b': nrm(ks[20], (DEPTH, D), 0.02),
        'router_w': nrm(ks[21], (DEPTH, D, N_EXPERTS), D ** -0.5),
        'router_b': nrm(ks[22], (DEPTH, N_EXPERTS), 0.01),
        'exp_w1': nrm(ks[23], (DEPTH, N_EXPERTS, D, 2 * D_FF), D ** -0.5),
        'exp_b1': nrm(ks[24], (DEPTH, N_EXPERTS, 2 * D_FF), 0.02),
        'exp_w2': nrm(ks[25], (DEPTH, N_EXPERTS, D_FF, D), DEEPNORM_BETA * D_FF ** -0.5),
        'exp_b2': nrm(ks[26], (DEPTH, N_EXPERTS, D), 0.02),
        'ln2_g': 1.0 + nrm(ks[27], (DEPTH, D), 0.02),
        'ln2_b': nrm(ks[28], (DEPTH, D), 0.02),
    }


def reference(x, c, ctx, c_ctx, w_ada, b_ada, w_in, conv_dw_w, conv_dw_b, conv_ln_g, conv_ln_b,
              lam_q1, lam_k1, lam_q2, lam_k2, diff_subln_g, nat_rpb, w_branch, w_out, ln1_g, ln1_b,
              router_w, router_b, exp_w1, exp_b1, exp_w2, exp_b2, ln2_g, ln2_b):
    b, l, d = x.shape
    n_ctx = ctx.shape[1]
    xc = ctx
    for i in range(DEPTH):
        with_ctx = i < DEPTH - 1
        lam_init = 0.8 - 0.6 * math.exp(-0.3 * i)
        lam = (jnp.exp(jnp.sum(lam_q1[i].astype(jnp.float32) * lam_k1[i].astype(jnp.float32)))
               - jnp.exp(jnp.sum(lam_q2[i].astype(jnp.float32) * lam_k2[i].astype(jnp.float32))) + lam_init)
        mod_l = (jax.nn.silu(c) @ w_ada[i] + b_ada[i])[:, None, :]
        mod_c = (jax.nn.silu(c_ctx) @ w_ada[i] + b_ada[i])[None, None, :]
        sh1, sc1, g1, sh2, sc2, g2 = jnp.split(mod_l, 6, axis=-1)
        csh1, csc1, cg1, csh2, csc2, cg2 = jnp.split(mod_c, 6, axis=-1)

        y_lat, y_ctx = mixing_sublayer(modulate(x, sh1, sc1), modulate(xc, csh1, csc1), w_in[i],
                                       conv_dw_w[i], conv_dw_b[i], conv_ln_g[i], conv_ln_b[i], lam, lam_init,
                                       diff_subln_g[i], nat_rpb[i], w_branch[i], w_out[i], with_ctx)
        x = layer_norm(DEEPNORM_ALPHA * x + g1 * y_lat, ln1_g[i], ln1_b[i])
        if with_ctx:
            xc = layer_norm(DEEPNORM_ALPHA * xc + cg1 * y_ctx, ln1_g[i], ln1_b[i])

        tok = modulate(x, sh2, sc2).reshape(b * l, d)
        if with_ctx:
            tok = jnp.concatenate([tok, modulate(xc, csh2, csc2).reshape(b * n_ctx, d)], axis=0)
        y = routed_moe(tok, router_w[i], router_b[i], exp_w1[i], exp_b1[i], exp_w2[i], exp_b2[i])
        x = layer_norm(DEEPNORM_ALPHA * x + g2 * y[:b * l].reshape(b, l, d), ln2_g[i], ln2_b[i])
        if with_ctx:
            xc = layer_norm(DEEPNORM_ALPHA * xc + cg2 * y[b * l:].reshape(b, n_ctx, d), ln2_g[i], ln2_b[i])
    return x
```

```python
import functools
import math

import numpy as np
import jax
import jax.numpy as jnp
from jax import lax
from jax.experimental import pallas as pl
from jax.experimental.pallas import tpu as pltpu

F32 = jnp.float32
BF16 = jnp.bfloat16
I32 = jnp.int32

GRID_W = 64
CONV_K = 31
DIFF_D = 64
HEAD_DIM = 64
NAT_KR = 8
NAT_KC = 16
N_EXPERTS = 32
TOP_K = 4
SWIGLU_LIMIT = 7.0
SWIGLU_ALPHA = 1.702
ROPE_BASE = 10000.0
LN_EPS = 1e-6
NEG_BIG = -1e30

LANES = 128
SUBLANES = 8
VMEM_LIMIT = 56 * 1024 * 1024

NAT_QROWS = 4
NAT_SPAN_ROWS = NAT_QROWS + NAT_KR
MOE_TILE = 512


def _sigmoid(x):
    return 1.0 / (1.0 + jnp.exp(-x))


def _ln(x):
    mu = jnp.mean(x, axis=-1, keepdims=True)
    xc = x - mu
    var = jnp.mean(xc * xc, axis=-1, keepdims=True)
    return xc * lax.rsqrt(var + LN_EPS)


def _dot(a, b):
    return jnp.dot(a, b, preferred_element_type=F32)


def _dot_nt(a, b):
    return lax.dot_general(a, b, (((1,), (1,)), ((), ())), preferred_element_type=F32)


def _params(*sem):
    return pltpu.CompilerParams(dimension_semantics=sem, vmem_limit_bytes=VMEM_LIMIT)


def _ada_kernel(cc_ref, w_ref, b_ref, o_ref):
    cc = cc_ref[...]
    s = cc * _sigmoid(cc)
    o_ref[0] = _dot(s.astype(BF16), w_ref[0].astype(BF16)) + b_ref[0]


def _ada(cc, w_ada, b_ada):
    depth, d, n = w_ada.shape
    rows = cc.shape[0]
    tn = 1536
    return pl.pallas_call(
        _ada_kernel,
        grid=(depth, n // tn),
        in_specs=[pl.BlockSpec((rows, d), lambda l, j: (0, 0)),
                  pl.BlockSpec((1, d, tn), lambda l, j: (l, 0, j)),
                  pl.BlockSpec((1, 1, tn), lambda l, j: (l, 0, j))],
        out_specs=pl.BlockSpec((1, rows, tn), lambda l, j: (l, 0, j)),
        out_shape=jax.ShapeDtypeStruct((depth, rows, n), F32),
        compiler_params=_params("parallel", "parallel"),
        name="ada",
    )(cc, w_ada, b_ada.reshape(depth, 1, n))


def _inproj_kernel(x_ref, mod_ref, w_ref, u_ref, h_scr, *, d):
    @pl.when(pl.program_id(1) == 0)
    def _():
        m = mod_ref[0]
        h = _ln(x_ref[...]) * (1.0 + m[:, d:2 * d]) + m[:, 0:d]
        h_scr[...] = h.astype(BF16)

    u_ref[...] = _dot(h_scr[...], w_ref[...]).astype(BF16)


def _inproj(xa, mod3, w_bf, mod_row, *, tm, tn):
    t, d = xa.shape
    n = w_bf.shape[1]
    return pl.pallas_call(
        functools.partial(_inproj_kernel, d=d),
        grid=(t // tm, n // tn),
        in_specs=[pl.BlockSpec((tm, d), lambda i, j: (i, 0)),
                  pl.BlockSpec((1, 1, 6 * d), lambda i, j: (mod_row(i, tm), 0, 0)),
                  pl.BlockSpec((d, tn), lambda i, j: (0, j))],
        out_specs=pl.BlockSpec((tm, tn), lambda i, j: (i, j)),
        out_shape=jax.ShapeDtypeStruct((t, n), BF16),
        scratch_shapes=[pltpu.VMEM((tm, d), BF16)],
        compiler_params=_params("parallel", "arbitrary"),
        name="inproj",
    )(xa, mod3, w_bf)


CONV_PAD = 16
CONV_ROWS = 64


def _conv_kernel(val_ref, gate_ref, w_ref, b_ref, g_ref, beta_ref, o_ref, hp_scr, *, seq, width):
    zeros = jnp.zeros((CONV_PAD, width), F32)
    hp_scr[0:CONV_PAD, :] = zeros
    hp_scr[seq + CONV_PAD:seq + 2 * CONV_PAD, :] = zeros
    hp_scr[CONV_PAD:seq + CONV_PAD, :] = val_ref[...].astype(F32) * _sigmoid(gate_ref[...].astype(F32))
    w = w_ref[...]
    shift = CONV_PAD - CONV_K // 2

    def chunk(ci, carry):
        t0 = pl.multiple_of(ci * CONV_ROWS, CONV_ROWS)
        strips = []
        for s in range(width // LANES):
            win = hp_scr[pl.ds(t0, CONV_ROWS + 2 * CONV_PAD), s * LANES:(s + 1) * LANES]
            acc = jnp.zeros((CONV_ROWS, LANES), F32)
            for k in range(CONV_K):
                acc = acc + w[k:k + 1, s * LANES:(s + 1) * LANES] * win[k + shift:k + shift + CONV_ROWS, :]
            strips.append(acc)
        y = jnp.concatenate(strips, axis=1) + b_ref[...]
        y = _ln(y) * g_ref[...] + beta_ref[...]
        o_ref[pl.ds(t0, CONV_ROWS), :] = (y * _sigmoid(y)).astype(BF16)
        return carry

    lax.fori_loop(0, seq // CONV_ROWS, chunk, 0)


def _conv(u, dw_w, dw_b, ln_g, ln_b, *, seq, nseq, row_block0):
    width = dw_w.shape[1]
    wpad = jnp.zeros((32, width), F32).at[:CONV_K].set(dw_w)
    const2 = lambda b: (0, 0)
    return pl.pallas_call(
        functools.partial(_conv_kernel, seq=seq, width=width),
        grid=(nseq,),
        in_specs=[pl.BlockSpec((seq, width), lambda b: (row_block0 + b, 0)),
                  pl.BlockSpec((seq, width), lambda b: (row_block0 + b, 1)),
                  pl.BlockSpec((32, width), const2),
                  pl.BlockSpec((1, width), const2), pl.BlockSpec((1, width), const2),
                  pl.BlockSpec((1, width), const2)],
        out_specs=pl.BlockSpec((seq, width), lambda b: (b, 0)),
        out_shape=jax.ShapeDtypeStruct((nseq * seq, width), BF16),
        scratch_shapes=[pltpu.VMEM((seq + 2 * CONV_PAD, width), F32)],
        compiler_params=_params("parallel"),
        name="conv",
    )(u, u, wpad, dw_b.reshape(1, width), ln_g.reshape(1, width), ln_b.reshape(1, width))


def _rope_kernel(q_ref, k_ref, cos_ref, sin_ref, qo_ref, ko_ref, *, scale):
    cos = cos_ref[...]
    sin = sin_ref[...]
    width = cos.shape[1]
    lane = lax.broadcasted_iota(I32, cos.shape, 1)
    first = (lane & (DIFF_D // 2 - 1)) < (DIFF_D // 4)

    def rot(x):
        partner = jnp.where(first, pltpu.roll(x, width - DIFF_D // 4, 1), pltpu.roll(x, DIFF_D // 4, 1))
        return x * cos + partner * sin

    qo_ref[...] = (rot(q_ref[...].astype(F32)) * scale).astype(BF16)
    ko_ref[...] = rot(k_ref[...].astype(F32)).astype(BF16)


def _rope(u, cos, sin, *, nlat, seq, tm, qcol, kcol):
    width = cos.shape[1]
    per_seq = seq // tm
    spec_tab = pl.BlockSpec((tm, width), lambda i: (i % per_seq, 0))
    out = jax.ShapeDtypeStruct((nlat, width), BF16)
    return pl.pallas_call(
        functools.partial(_rope_kernel, scale=DIFF_D ** -0.5),
        grid=(nlat // tm,),
        in_specs=[pl.BlockSpec((tm, width), lambda i: (i, qcol)),
                  pl.BlockSpec((tm, width), lambda i: (i, kcol)),
                  spec_tab, spec_tab],
        out_specs=[pl.BlockSpec((tm, width), lambda i: (i, 0))] * 2,
        out_shape=[out, out],
        compiler_params=_params("parallel"),
        name="rope",
    )(u, u, cos, sin)


def _rope_tables(seq, n_maps):
    axis_dim = DIFF_D // 2
    inv_freq = ROPE_BASE ** (-jnp.arange(0, axis_dim, 2, dtype=F32) / axis_dim)
    t = jnp.arange(seq, dtype=I32)
    ang_r = (t // GRID_W).astype(F32)[:, None] * inv_freq[None, :]
    ang_c = (t % GRID_W).astype(F32)[:, None] * inv_freq[None, :]
    cos = jnp.concatenate([jnp.cos(ang_r)] * 2 + [jnp.cos(ang_c)] * 2, axis=1)
    sin = jnp.concatenate([-jnp.sin(ang_r), jnp.sin(ang_r), -jnp.sin(ang_c), jnp.sin(ang_c)], axis=1)
    return jnp.tile(cos, (1, n_maps)), jnp.tile(sin, (1, n_maps))


def _lambda(lam_ref, lam_init):
    lv = lam_ref[...]
    return (jnp.exp(jnp.sum(lv[0:1] * lv[1:2], axis=-1, keepdims=True))
            - jnp.exp(jnp.sum(lv[2:3] * lv[3:4], axis=-1, keepdims=True)) + lam_init)


def _softmax_parts(score_parts):
    m = score_parts[0].max(axis=-1, keepdims=True)
    for s in score_parts[1:]:
        m = jnp.maximum(m, s.max(axis=-1, keepdims=True))
    es = [jnp.exp(s - m) for s in score_parts]
    tot = es[0].sum(axis=-1, keepdims=True)
    for e in es[1:]:
        tot = tot + e.sum(axis=-1, keepdims=True)
    return es, 1.0 / tot


def _diff_head(q1, q2, keys1, keys2, vals, lam, g, lam_init):
    e1, inv1 = _softmax_parts([_dot_nt(q1, k) for k in keys1])
    e2, inv2 = _softmax_parts([_dot_nt(q2, k) for k in keys2])
    o = None
    for a, b, v in zip(e1, e2, vals):
        p = (a * inv1 - lam * (b * inv2)).astype(BF16)
        o = _dot(p, v) if o is None else o + _dot(p, v)
    o = o * lax.rsqrt(jnp.mean(o * o, axis=-1, keepdims=True) + LN_EPS)
    return o * g * (1.0 - lam_init)


def _diff_kernel(q_ref, k_ref, v_ref, kc_ref, vc_ref, lam_ref, g_ref, o_ref, *, lam_init, heads):
    lam = _lambda(lam_ref, lam_init)
    g = g_ref[...]
    hd = 2 * DIFF_D
    for h in range(heads):
        c0 = h * hd
        o = _diff_head(q_ref[:, c0:c0 + DIFF_D], q_ref[:, c0 + DIFF_D:c0 + hd],
                       [kc_ref[:, c0:c0 + DIFF_D], k_ref[:, c0:c0 + DIFF_D]],
                       [kc_ref[:, c0 + DIFF_D:c0 + hd], k_ref[:, c0 + DIFF_D:c0 + hd]],
                       [vc_ref[:, c0:c0 + hd], v_ref[:, c0:c0 + hd]], lam, g, lam_init)
        o_ref[:, c0:c0 + hd] = o.astype(BF16)


def _diff_attn(q_rot, k_rot, u, lam_vecs, subln_g, *, batch, seq, n_ctx, tq, lam_init, kcol, vcol):
    width = q_rot.shape[1]
    nq = seq // tq
    ctx_blk0 = batch * seq // n_ctx
    return pl.pallas_call(
        functools.partial(_diff_kernel, lam_init=lam_init, heads=width // (2 * DIFF_D)),
        grid=(batch, nq),
        in_specs=[pl.BlockSpec((tq, width), lambda b, i: (b * nq + i, 0)),
                  pl.BlockSpec((seq, width), lambda b, i: (b, 0)),
                  pl.BlockSpec((seq, width), lambda b, i: (b, vcol)),
                  pl.BlockSpec((n_ctx, width), lambda b, i: (ctx_blk0 + b, kcol)),
                  pl.BlockSpec((n_ctx, width), lambda b, i: (ctx_blk0 + b, vcol)),
                  pl.BlockSpec((4, DIFF_D), lambda b, i: (0, 0)),
                  pl.BlockSpec((1, 2 * DIFF_D), lambda b, i: (0, 0))],
        out_specs=pl.BlockSpec((tq, width), lambda b, i: (b * nq + i, 0)),
        out_shape=jax.ShapeDtypeStruct((batch * seq, width), BF16),
        compiler_params=_params("parallel", "arbitrary"),
        name="diff_attn",
    )(q_rot, k_rot, u, u, u, lam_vecs, subln_g.reshape(1, 2 * DIFF_D))


def _ctx_attn_kernel(bq_ref, bk_ref, bv_ref, cq_ref, ck_ref, cv_ref, lam_ref, g_ref, yb_ref, yc_ref, *, lam_init):
    lam = _lambda(lam_ref, lam_init)
    g = g_ref[...]
    width = bq_ref.shape[1]
    hd = 2 * DIFF_D
    scale = jnp.asarray(DIFF_D ** -0.5, BF16)
    for h in range(width // hd):
        c0 = h * hd
        o = _diff_head(bq_ref[:, c0:c0 + DIFF_D] * scale, bq_ref[:, c0 + DIFF_D:c0 + hd] * scale,
                       [bk_ref[:, c0:c0 + DIFF_D]], [bk_ref[:, c0 + DIFF_D:c0 + hd]],
                       [bv_ref[:, c0:c0 + hd]], lam, g, lam_init)
        yb_ref[:, c0:c0 + hd] = o.astype(BF16)
    scale_c = jnp.asarray(HEAD_DIM ** -0.5, BF16)
    for h in range(width // HEAD_DIM):
        c0 = h * HEAD_DIM
        es, inv = _softmax_parts([_dot_nt(cq_ref[:, c0:c0 + HEAD_DIM] * scale_c, ck_ref[:, c0:c0 + HEAD_DIM])])
        o = _dot((es[0] * inv).astype(BF16), cv_ref[:, c0:c0 + HEAD_DIM])
        yc_ref[:, c0:c0 + HEAD_DIM] = o.astype(BF16)


def _ctx_attn(u, lam_vecs, subln_g, *, batch, seq, n_ctx, width, lam_init, cols):
    blk0 = batch * seq // n_ctx
    u_specs = [pl.BlockSpec((n_ctx, width), functools.partial(lambda b, col: (blk0 + b, col), col=col))
               for col in cols]
    out_spec = pl.BlockSpec((n_ctx, width), lambda b: (b, 0))
    out = jax.ShapeDtypeStruct((batch * n_ctx, width), BF16)
    return pl.pallas_call(
        functools.partial(_ctx_attn_kernel, lam_init=lam_init),
        grid=(batch,),
        in_specs=u_specs + [pl.BlockSpec((4, DIFF_D), lambda b: (0, 0)),
                            pl.BlockSpec((1, 2 * DIFF_D), lambda b: (0, 0))],
        out_specs=[out_spec, out_spec],
        out_shape=[out, out],
        compiler_params=_params("parallel"),
        name="ctx_attn",
    )(u, u, u, u, u, u, lam_vecs, subln_g.reshape(1, 2 * DIFF_D))


def _nat_kernel(q_ref, k_ref, v_ref, kc_ref, vc_ref, bias_ref, o_ref, *, rows):
    j = pl.program_id(1)
    span = NAT_SPAN_ROWS * GRID_W
    start_row = jnp.clip(j * NAT_QROWS - NAT_KR // 2, 0, rows - NAT_SPAN_ROWS)
    start = pl.multiple_of(start_row * GRID_W, NAT_QROWS * GRID_W)
    scale = jnp.asarray(HEAD_DIM ** -0.5, BF16)
    for h in range(q_ref.shape[1] // HEAD_DIM):
        c0 = h * HEAD_DIM
        q = q_ref[:, c0:c0 + HEAD_DIM] * scale
        s_loc = _dot_nt(q, k_ref[pl.ds(start, span), c0:c0 + HEAD_DIM]) + bias_ref[0, h]
        s_ctx = _dot_nt(q, kc_ref[:, c0:c0 + HEAD_DIM])
        (e_loc, e_ctx), inv = _softmax_parts([s_loc, s_ctx])
        o = (_dot(e_loc.astype(BF16), v_ref[pl.ds(start, span), c0:c0 + HEAD_DIM])
             + _dot(e_ctx.astype(BF16), vc_ref[:, c0:c0 + HEAD_DIM]))
        o_ref[:, c0:c0 + HEAD_DIM] = (o * inv).astype(BF16)


def _nat_bias_tables(rpb, rows):
    n_groups = rows // NAT_QROWS
    tabs = []
    for j in (0, 1, n_groups - 1):
        start = int(np.clip(j * NAT_QROWS - NAT_KR // 2, 0, rows - NAT_SPAN_ROWS))
        r = j * NAT_QROWS + np.arange(NAT_QROWS)
        kr = start + np.arange(NAT_SPAN_ROWS)
        rs = np.clip(r - NAT_KR // 2, 0, rows - NAT_KR)
        row_ok = (kr[None, :] >= rs[:, None]) & (kr[None, :] < rs[:, None] + NAT_KR)
        d_row = np.clip(kr[None, :] - r[:, None] + NAT_KR - 1, 0, 2 * NAT_KR - 2)
        c = np.arange(GRID_W)
        cs = np.clip(c - NAT_KC // 2, 0, GRID_W - NAT_KC)
        col_ok = (c[None, :] >= cs[:, None]) & (c[None, :] < cs[:, None] + NAT_KC)
        d_col = np.clip(c[None, :] - c[:, None] + NAT_KC - 1, 0, 2 * NAT_KC - 2)
        t = rpb[:, d_row][:, :, :, d_col]
        ok = row_ok[:, :, None, None] & col_ok[None, None, :, :]
        t = jnp.where(ok[None], t.astype(F32), NEG_BIG)
        t = t.transpose(0, 1, 3, 2, 4).reshape(rpb.shape[0], NAT_QROWS * GRID_W, NAT_SPAN_ROWS * GRID_W)
        tabs.append(t)
    return jnp.stack(tabs)


def _nat(u, bias, *, batch, seq, n_ctx, qcol, kcol, vcol):
    width = 8 * HEAD_DIM
    rows = seq // GRID_W
    n_groups = rows // NAT_QROWS
    tq = NAT_QROWS * GRID_W
    ctx_blk0 = batch * seq // n_ctx
    heads = bias.shape[1]

    def bias_idx(b, j):
        return (jnp.where(j == 0, 0, jnp.where(j == n_groups - 1, 2, 1)), 0, 0, 0)

    return pl.pallas_call(
        functools.partial(_nat_kernel, rows=rows),
        grid=(batch, n_groups),
        in_specs=[pl.BlockSpec((tq, width), lambda b, j: (b * n_groups + j, qcol)),
                  pl.BlockSpec((seq, width), lambda b, j: (b, kcol)),
                  pl.BlockSpec((seq, width), lambda b, j: (b, vcol)),
                  pl.BlockSpec((n_ctx, width), lambda b, j: (ctx_blk0 + b, kcol)),
                  pl.BlockSpec((n_ctx, width), lambda b, j: (ctx_blk0 + b, vcol)),
                  pl.BlockSpec((1, heads, tq, NAT_SPAN_ROWS * GRID_W), bias_idx)],
        out_specs=pl.BlockSpec((tq, width), lambda b, j: (b * n_groups + j, 0)),
        out_shape=jax.ShapeDtypeStruct((batch * seq, width), BF16),
        compiler_params=_params("parallel", "arbitrary"),
        name="nat",
    )(u, u, u, u, u, bias)


def _merge_kernel(ya_ref, yb_ref, yc_ref, ya_ctx_ref, yb_ctx_ref, yc_ctx_ref, ga_ref, gb_ref, gc_ref, x_ref, mod_ref,
                  wb_ref, wo_ref, lg_ref, lb_ref, x1_ref, tok_ref, *, d, alpha, lat_tiles):
    m = mod_ref[0]
    is_ctx = pl.program_id(0) >= lat_tiles
    ya = jnp.where(is_ctx, ya_ctx_ref[...], ya_ref[...])
    yb = jnp.where(is_ctx, yb_ctx_ref[...], yb_ref[...])
    yc = jnp.where(is_ctx, yc_ctx_ref[...], yc_ref[...])
    acc = _sigmoid(ga_ref[...].astype(F32)) * _dot(ya, wb_ref[0])
    acc = acc + _sigmoid(gb_ref[...].astype(F32)) * _dot(yb, wb_ref[1])
    acc = acc + _sigmoid(gc_ref[...].astype(F32)) * _dot(yc, wb_ref[2])
    y = _dot(acc.astype(BF16), wo_ref[...])
    x1 = _ln(alpha * x_ref[...] + m[:, 2 * d:3 * d] * y) * lg_ref[...] + lb_ref[...]
    x1_ref[...] = x1
    tok_ref[...] = _ln(x1) * (1.0 + m[:, 4 * d:5 * d]) + m[:, 3 * d:4 * d]


def _merge(y_lat, y_ctx, u, xa, mod3, wb_bf, wo_bf, ln_g, ln_b, mod_row, *, n_tok, tm, gcol0, alpha):
    d = xa.shape[1]
    half = y_lat[0].shape[1]
    lat_tiles = y_lat[0].shape[0] // tm
    ctx_tiles = y_ctx[0].shape[0] // tm
    tile = lambda i: (i, 0)
    const2 = lambda i: (0, 0)
    lat_tile = lambda i: (jnp.minimum(i, lat_tiles - 1), 0)
    ctx_tile = lambda i: (jnp.clip(i - lat_tiles, 0, ctx_tiles - 1), 0)
    out = jax.ShapeDtypeStruct((n_tok, d), F32)
    return pl.pallas_call(
        functools.partial(_merge_kernel, d=d, alpha=alpha, lat_tiles=lat_tiles),
        grid=(n_tok // tm,),
        in_specs=[pl.BlockSpec((tm, half), lat_tile)] * 3 + [pl.BlockSpec((tm, half), ctx_tile)] * 3
                 + [pl.BlockSpec((tm, d), functools.partial(lambda i, col: (i, col), col=gcol0 + k)) for k in range(3)]
                 + [pl.BlockSpec((tm, d), tile),
                    pl.BlockSpec((1, 1, 6 * d), lambda i: (mod_row(i, tm), 0, 0)),
                    pl.BlockSpec((3, half, d), lambda i: (0, 0, 0)),
                    pl.BlockSpec((d, d), const2),
                    pl.BlockSpec((1, d), const2), pl.BlockSpec((1, d), const2)],
        out_specs=[pl.BlockSpec((tm, d), tile)] * 2,
        out_shape=[out, out],
        compiler_params=_params("parallel"),
        name="merge",
    )(*y_lat, *y_ctx, u, u, u, xa, mod3, wb_bf, wo_bf, ln_g.reshape(1, d), ln_b.reshape(1, d))


def _router_kernel(tok_ref, w_ref, b_ref, idx_ref, wts_ref, rank_ref, cnt_ref, carry_scr):
    i = pl.program_id(0)

    @pl.when(i == 0)
    def _():
        carry_scr[...] = jnp.zeros_like(carry_scr)

    tm = tok_ref.shape[0]
    logits = _dot(tok_ref[...].astype(BF16), w_ref[...]) + b_ref[...]
    lane = lax.broadcasted_iota(I32, logits.shape, 1).astype(F32)
    work = logits
    idxs, vals = [], []
    for _ in range(TOP_K):
        m = work.max(axis=-1, keepdims=True)
        sel = jnp.min(jnp.where(work == m, lane, float(LANES)), axis=-1, keepdims=True)
        idxs.append(sel)
        vals.append(m)
        work = jnp.where(lane == sel, -jnp.inf, work)
    es = [jnp.exp(v - vals[0]) for v in vals]
    inv = 1.0 / (es[0] + es[1] + es[2] + es[3])

    onehot = jnp.zeros(logits.shape, F32)
    for sel in idxs:
        onehot = onehot + jnp.where(lane == sel, 1.0, 0.0)
    r_i = lax.broadcasted_iota(I32, (tm, tm), 0)
    c_i = lax.broadcasted_iota(I32, (tm, tm), 1)
    tri = jnp.where(c_i < r_i, 1.0, 0.0).astype(BF16)
    before = carry_scr[...] + _dot(tri, onehot.astype(BF16))

    idx_out = jnp.zeros(logits.shape, F32)
    wts_out = jnp.zeros(logits.shape, F32)
    rank_out = jnp.zeros(logits.shape, F32)
    for k in range(TOP_K):
        rank_k = jnp.sum(jnp.where(lane == idxs[k], before, 0.0), axis=-1, keepdims=True)
        idx_out = jnp.where(lane == float(k), idxs[k], idx_out)
        wts_out = jnp.where(lane == float(k), es[k] * inv, wts_out)
        rank_out = jnp.where(lane == float(k), rank_k, rank_out)
    idx_ref[...] = idx_out.astype(I32)
    wts_ref[...] = wts_out
    rank_ref[...] = rank_out.astype(I32)
    carry_scr[...] = carry_scr[...] + jnp.sum(onehot, axis=0, keepdims=True)
    cnt_ref[...] = carry_scr[...]


def _router(tok, w_r, b_r, *, tm):
    t, d = tok.shape
    w_pad = jnp.zeros((d, LANES), BF16).at[:, :N_EXPERTS].set(w_r.astype(BF16))
    b_pad = jnp.full((1, LANES), NEG_BIG, F32).at[0, :N_EXPERTS].set(b_r)
    tile = lambda i: (i, 0)
    const2 = lambda i: (0, 0)
    return pl.pallas_call(
        _router_kernel,
        grid=(t // tm,),
        in_specs=[pl.BlockSpec((tm, d), tile), pl.BlockSpec((d, LANES), const2), pl.BlockSpec((1, LANES), const2)],
        out_specs=[pl.BlockSpec((tm, LANES), tile)] * 3 + [pl.BlockSpec((1, LANES), const2)],
        out_shape=[jax.ShapeDtypeStruct((t, LANES), I32), jax.ShapeDtypeStruct((t, LANES), F32),
                   jax.ShapeDtypeStruct((t, LANES), I32), jax.ShapeDtypeStruct((1, LANES), F32)],
        scratch_shapes=[pltpu.VMEM((1, LANES), F32)],
        compiler_params=_params("arbitrary"),
        name="router",
    )(tok, w_pad, b_pad)


def _dispatch_kernel(pend_ref, padded_ref, slot_ref, tok_ref, xs_ref, zero_scr, sem, zsem, *, tile):
    i = pl.program_id(0)
    tm = tok_ref.shape[0]

    def zero_copy(e):
        off = pl.multiple_of(jnp.maximum(pend_ref[e] - tile, 0), tile)
        return pltpu.make_async_copy(zero_scr, xs_ref.at[pl.ds(off, tile), :], zsem)

    def tail_copy(j):
        return pltpu.make_async_copy(zero_scr, xs_ref.at[pl.ds(pl.multiple_of(j * tile, tile), tile), :], zsem)

    @pl.when(i == 0)
    def _():
        zero_scr[...] = jnp.zeros_like(zero_scr)
        n_used = pend_ref[N_EXPERTS - 1] // tile
        n_tiles = xs_ref.shape[0] // tile
        for e in range(N_EXPERTS):
            @pl.when(padded_ref[e] > 0)
            def _():
                zero_copy(e).start()

        def start_tail(j, carry):
            tail_copy(j).start()
            return carry

        def wait_tail(j, carry):
            tail_copy(j).wait()
            return carry

        lax.fori_loop(n_used, n_tiles, start_tail, 0)
        for e in range(N_EXPERTS):
            @pl.when(padded_ref[e] > 0)
            def _():
                zero_copy(e).wait()
        lax.fori_loop(n_used, n_tiles, wait_tail, 0)

    def row_copy(r, k):
        return pltpu.make_async_copy(tok_ref.at[pl.ds(r, 1), :],
                                     xs_ref.at[pl.ds(slot_ref[0, 0, r * TOP_K + k], 1), :], sem)

    def body(r, carry):
        for k in range(TOP_K):
            row_copy(r, k).start()
        return carry

    lax.fori_loop(0, tm, body, 0)

    def drain(r, carry):
        for k in range(TOP_K):
            row_copy(r, k).wait()
        return carry

    lax.fori_loop(0, tm, drain, 0)


def _dispatch(tok, slot, pad_end, padded, *, tm, tile, n_slots):
    t, d = tok.shape
    slot3 = slot.reshape(t // tm, 1, tm * TOP_K)
    grid_spec = pltpu.PrefetchScalarGridSpec(
        num_scalar_prefetch=2,
        grid=(t // tm,),
        in_specs=[pl.BlockSpec((1, 1, tm * TOP_K), lambda i, pe, pd: (i, 0, 0), memory_space=pltpu.SMEM),
                  pl.BlockSpec((tm, d), lambda i, pe, pd: (i, 0))],
        out_specs=pl.BlockSpec(memory_space=pl.ANY),
        scratch_shapes=[pltpu.VMEM((tile, d), F32), pltpu.SemaphoreType.DMA(()), pltpu.SemaphoreType.DMA(())],
    )
    return pl.pallas_call(
        functools.partial(_dispatch_kernel, tile=tile),
        grid_spec=grid_spec,
        out_shape=jax.ShapeDtypeStruct((n_slots, d), F32),
        compiler_params=_params("arbitrary"),
        name="dispatch",
    )(pad_end, padded, slot3, tok)


def _expert_kernel(te_ref, nu_ref, xs_ref, w1_ref, b1_ref, w2_ref, b2_ref, ys_ref, *, ff):
    del te_ref

    @pl.when(pl.program_id(0) >= nu_ref[0])
    def _():
        ys_ref[...] = jnp.zeros_like(ys_ref)

    @pl.when(pl.program_id(0) < nu_ref[0])
    def _():
        hh = _dot(xs_ref[...].astype(BF16), w1_ref[0]) + b1_ref[0]
        glu = jnp.minimum(hh[:, :ff], SWIGLU_LIMIT)
        lin = jnp.clip(hh[:, ff:], -SWIGLU_LIMIT, SWIGLU_LIMIT)
        act = glu * _sigmoid(SWIGLU_ALPHA * glu) * (lin + 1.0)
        ys_ref[...] = _dot(act.astype(BF16), w2_ref[0]) + b2_ref[0]


def _experts(xs, w1_bf, b1, w2_bf, b2, tile_expert, n_used, *, tile):
    n_slots, d = xs.shape
    ne, _, ff2 = w1_bf.shape
    ff = ff2 // 2
    row = lambda i, te, nu: (jnp.minimum(i, nu[0] - 1), 0)
    per_e = lambda i, te, nu: (te[i], 0, 0)
    grid_spec = pltpu.PrefetchScalarGridSpec(
        num_scalar_prefetch=2,
        grid=(n_slots // tile,),
        in_specs=[pl.BlockSpec((tile, d), row),
                  pl.BlockSpec((1, d, ff2), per_e), pl.BlockSpec((1, 1, ff2), per_e),
                  pl.BlockSpec((1, ff, d), per_e), pl.BlockSpec((1, 1, d), per_e)],
        out_specs=pl.BlockSpec((tile, d), lambda i, te, nu: (i, 0)),
    )
    return pl.pallas_call(
        functools.partial(_expert_kernel, ff=ff),
        grid_spec=grid_spec,
        out_shape=jax.ShapeDtypeStruct((n_slots, d), F32),
        compiler_params=_params("arbitrary"),
        name="experts",
    )(tile_expert, n_used, xs, w1_bf, b1.reshape(ne, 1, ff2), w2_bf, b2.reshape(ne, 1, d))


def _combine_kernel(slot_ref, wts_ref, x1_ref, mod_ref, lg_ref, lb_ref, ys_ref, o_ref, buf, sem, *, d, alpha):
    tm = x1_ref.shape[0]

    def row_copy(r, k):
        return pltpu.make_async_copy(ys_ref.at[pl.ds(slot_ref[0, 0, r * TOP_K + k], 1), :],
                                     buf.at[k, pl.ds(r, 1), :], sem)

    def body(r, carry):
        for k in range(TOP_K):
            row_copy(r, k).start()
        return carry

    lax.fori_loop(0, tm, body, 0)

    def drain(r, carry):
        for k in range(TOP_K):
            row_copy(r, k).wait()
        return carry

    lax.fori_loop(0, tm, drain, 0)

    w = wts_ref[...]
    y = w[:, 0:1] * buf[0]
    for k in range(1, TOP_K):
        y = y + w[:, k:k + 1] * buf[k]
    m = mod_ref[0]
    o_ref[...] = _ln(alpha * x1_ref[...] + m[:, 5 * d:6 * d] * y) * lg_ref[...] + lb_ref[...]


def _combine(ys, slot, wts, x1, mod3, ln_g, ln_b, mod_row, *, tm, alpha):
    t, d = x1.shape
    slot3 = slot.reshape(t // tm, 1, tm * TOP_K)
    tile = lambda i: (i, 0)
    const2 = lambda i: (0, 0)
    return pl.pallas_call(
        functools.partial(_combine_kernel, d=d, alpha=alpha),
        grid=(t // tm,),
        in_specs=[pl.BlockSpec((1, 1, tm * TOP_K), lambda i: (i, 0, 0), memory_space=pltpu.SMEM),
                  pl.BlockSpec((tm, LANES), tile),
                  pl.BlockSpec((tm, d), tile),
                  pl.BlockSpec((1, 1, 6 * d), lambda i: (mod_row(i, tm), 0, 0)),
                  pl.BlockSpec((1, d), const2), pl.BlockSpec((1, d), const2),
                  pl.BlockSpec(memory_space=pl.ANY)],
        out_specs=pl.BlockSpec((tm, d), tile),
        out_shape=jax.ShapeDtypeStruct((t, d), F32),
        scratch_shapes=[pltpu.VMEM((TOP_K, tm, d), F32), pltpu.SemaphoreType.DMA(())],
        compiler_params=_params("arbitrary"),
        name="combine",
    )(slot3, wts, x1, mod3, ln_g.reshape(1, d), ln_b.reshape(1, d), ys)


def _moe(tok, x1, mod3, mod_row, router_w, router_b, w1_bf, b1, w2_bf, b2, ln_g, ln_b, *, alpha):
    t = tok.shape[0]
    tile = MOE_TILE
    idx, wts, rank, counts = _router(tok, router_w, router_b, tm=math.gcd(512, t))
    counts = counts[0, :N_EXPERTS].astype(I32)
    padded = (counts + tile - 1) // tile * tile
    pad_end = jnp.cumsum(padded)
    pad_off = pad_end - padded
    top_i = idx[:, :TOP_K]
    slot = (pad_off[top_i] + rank[:, :TOP_K]).astype(I32)
    n_tiles = t * TOP_K // tile + N_EXPERTS
    n_used = (pad_end[-1] // tile).astype(I32).reshape(1)
    tile_start = jnp.minimum(jnp.arange(n_tiles, dtype=I32), n_used[0] - 1) * tile
    tile_expert = jnp.minimum(jnp.sum(pad_end[None, :] <= tile_start[:, None], axis=1), N_EXPERTS - 1).astype(I32)
    xs = _dispatch(tok, slot, pad_end.astype(I32), padded.astype(I32), tm=256, tile=tile, n_slots=n_tiles * tile)
    ys = _experts(xs, w1_bf, b1, w2_bf, b2, tile_expert, n_used, tile=tile)
    return _combine(ys, slot, wts, x1, mod3, ln_g, ln_b, mod_row, tm=256, alpha=alpha)


def kernel(x, c, ctx, c_ctx, w_ada, b_ada, w_in, conv_dw_w, conv_dw_b, conv_ln_g, conv_ln_b, lam_q1, lam_k1, lam_q2,
           lam_k2, diff_subln_g, nat_rpb, w_branch, w_out, ln1_g, ln1_b, router_w, router_b, exp_w1, exp_b1, exp_w2,
           exp_b2, ln2_g, ln2_b):
    batch, seq, d = x.shape
    n_ctx = ctx.shape[1]
    depth = w_ada.shape[0]
    half = d // 2
    nlat, nctx = batch * seq, batch * n_ctx
    alpha = (2 * depth) ** 0.25
    assert seq % (GRID_W * NAT_QROWS) == 0 and seq // GRID_W >= NAT_SPAN_ROWS and seq % n_ctx == 0

    xa = jnp.concatenate([x.reshape(nlat, d), ctx.reshape(nctx, d)], axis=0)
    mod_rows = -(-(batch + 1) // SUBLANES) * SUBLANES
    cc = jnp.zeros((mod_rows, d), F32).at[:batch].set(c).at[batch].set(c_ctx)
    mod3 = _ada(cc, w_ada, b_ada).reshape(depth * mod_rows, 1, 6 * d)
    cos, sin = _rope_tables(seq, half // DIFF_D)

    for i in range(depth):
        with_ctx = i < depth - 1
        lam_init = 0.8 - 0.6 * math.exp(-0.3 * i)
        n_tok = nlat + nctx if with_ctx else nlat

        def mod_row(t, tm, i=i):
            return i * mod_rows + jnp.where(t * tm < nlat, (t * tm) // seq, batch)

        u = _inproj(xa, mod3, w_in[i].astype(BF16), mod_row, tm=math.gcd(1024, nctx), tn=1792)

        conv_p = (conv_dw_w[i], conv_dw_b[i], conv_ln_g[i], conv_ln_b[i])
        ya = _conv(u, *conv_p, seq=seq, nseq=batch, row_block0=0)
        q_rot, k_rot = _rope(u, cos, sin, nlat=nlat, seq=seq, tm=1024, qcol=2, kcol=3)
        lam_vecs = jnp.stack([lam_q1[i], lam_k1[i], lam_q2[i], lam_k2[i]]).astype(F32)
        yb = _diff_attn(q_rot, k_rot, u, lam_vecs, diff_subln_g[i], batch=batch, seq=seq, n_ctx=n_ctx, tq=256,
                        lam_init=lam_init, kcol=3, vcol=4)
        bias = _nat_bias_tables(nat_rpb[i], seq // GRID_W)
        yc = _nat(u, bias, batch=batch, seq=seq, n_ctx=n_ctx, qcol=5, kcol=6, vcol=7)
        y_lat = (ya, yb, yc)
        y_ctx = y_lat
        if with_ctx:
            ya_c = _conv(u, *conv_p, seq=n_ctx, nseq=batch, row_block0=nlat // n_ctx)
            yb_c, yc_c = _ctx_attn(u, lam_vecs, diff_subln_g[i], batch=batch, seq=seq, n_ctx=n_ctx, width=half,
                                   lam_init=lam_init, cols=(2, 3, 4, 5, 6, 7))
            y_ctx = (ya_c, yb_c, yc_c)

        x1, tok = _merge(y_lat, y_ctx, u, xa, mod3, w_branch[i].astype(BF16), w_out[i].astype(BF16), ln1_g[i],
                         ln1_b[i], mod_row, n_tok=n_tok, tm=math.gcd(512, nctx), gcol0=4, alpha=alpha)
        xa_new = _moe(tok, x1, mod3, mod_row, router_w[i], router_b[i], exp_w1[i].astype(BF16), exp_b1[i],
                      exp_w2[i].astype(BF16), exp_b2[i], ln2_g[i], ln2_b[i], alpha=alpha)
        xa = xa_new
    return xa[:nlat].reshape(batch, seq, d)
```

```python
import functools
import math

import numpy as np
import jax
import jax.numpy as jnp
from jax import lax
from jax.experimental import pallas as pl
from jax.experimental.pallas import tpu as pltpu

F32 = jnp.float32
BF16 = jnp.bfloat16
I32 = jnp.int32

GRID_W = 64
CONV_K = 31
DIFF_D = 64
HEAD_DIM = 64
NAT_KR = 8
NAT_KC = 16
N_EXPERTS = 32
TOP_K = 4
SWIGLU_LIMIT = 7.0
SWIGLU_ALPHA = 1.702
ROPE_BASE = 10000.0
LN_EPS = 1e-6
NEG_BIG = -1e30

LANES = 128
SUBLANES = 8
VMEM_LIMIT = 56 * 1024 * 1024

NAT_QROWS = 4
NAT_SPAN_ROWS = NAT_QROWS + NAT_KR
MOE_TILE = 512
MOE_BLOCK = 256
CHUNK = SUBLANES


def _sigmoid(x):
    return 1.0 / (1.0 + jnp.exp(-x))


def _ln(x):
    mu = jnp.mean(x, axis=-1, keepdims=True)
    xc = x - mu
    var = jnp.mean(xc * xc, axis=-1, keepdims=True)
    return xc * lax.rsqrt(var + LN_EPS)


def _dot(a, b):
    return jnp.dot(a, b, preferred_element_type=F32)


def _dot_nt(a, b):
    return lax.dot_general(a, b, (((1,), (1,)), ((), ())), preferred_element_type=F32)


def _params(*sem):
    return pltpu.CompilerParams(dimension_semantics=sem, vmem_limit_bytes=VMEM_LIMIT)


def _ada_kernel(cc_ref, w_ref, b_ref, o_ref):
    cc = cc_ref[...]
    s = cc * _sigmoid(cc)
    o_ref[0] = _dot(s.astype(BF16), w_ref[0].astype(BF16)) + b_ref[0]


def _ada(cc, w_ada, b_ada):
    depth, d, n = w_ada.shape
    rows = cc.shape[0]
    tn = 1536
    return pl.pallas_call(
        _ada_kernel,
        grid=(depth, n // tn),
        in_specs=[pl.BlockSpec((rows, d), lambda l, j: (0, 0)),
                  pl.BlockSpec((1, d, tn), lambda l, j: (l, 0, j)),
                  pl.BlockSpec((1, 1, tn), lambda l, j: (l, 0, j))],
        out_specs=pl.BlockSpec((1, rows, tn), lambda l, j: (l, 0, j)),
        out_shape=jax.ShapeDtypeStruct((depth, rows, n), F32),
        compiler_params=_params("parallel", "parallel"),
        name="ada",
    )(cc, w_ada, b_ada.reshape(depth, 1, n))


def _inproj_kernel(x_ref, mod_ref, w_ref, u_ref, h_scr, *, d):
    @pl.when(pl.program_id(1) == 0)
    def _():
        m = mod_ref[0]
        h = _ln(x_ref[...]) * (1.0 + m[:, d:2 * d]) + m[:, 0:d]
        h_scr[...] = h.astype(BF16)

    u_ref[...] = _dot(h_scr[...], w_ref[...]).astype(BF16)


def _inproj(xa, mod3, w_bf, mod_row, *, tm, tn):
    t, d = xa.shape
    n = w_bf.shape[1]
    return pl.pallas_call(
        functools.partial(_inproj_kernel, d=d),
        grid=(t // tm, n // tn),
        in_specs=[pl.BlockSpec((tm, d), lambda i, j: (i, 0)),
                  pl.BlockSpec((1, 1, 6 * d), lambda i, j: (mod_row(i, tm), 0, 0)),
                  pl.BlockSpec((d, tn), lambda i, j: (0, j))],
        out_specs=pl.BlockSpec((tm, tn), lambda i, j: (i, j)),
        out_shape=jax.ShapeDtypeStruct((t, n), BF16),
        scratch_shapes=[pltpu.VMEM((tm, d), BF16)],
        compiler_params=_params("parallel", "arbitrary"),
        name="inproj",
    )(xa, mod3, w_bf)


CONV_PAD = 16
CONV_ROWS = 64


def _conv_kernel(val_ref, gate_ref, w_ref, b_ref, g_ref, beta_ref, o_ref, hp_scr, *, seq, width):
    zeros = jnp.zeros((CONV_PAD, width), F32)
    hp_scr[0:CONV_PAD, :] = zeros
    hp_scr[seq + CONV_PAD:seq + 2 * CONV_PAD, :] = zeros
    hp_scr[CONV_PAD:seq + CONV_PAD, :] = val_ref[...].astype(F32) * _sigmoid(gate_ref[...].astype(F32))
    w = w_ref[...]
    shift = CONV_PAD - CONV_K // 2

    def chunk(ci, carry):
        t0 = pl.multiple_of(ci * CONV_ROWS, CONV_ROWS)
        strips = []
        for s in range(width // LANES):
            win = hp_scr[pl.ds(t0, CONV_ROWS + 2 * CONV_PAD), s * LANES:(s + 1) * LANES]
            acc = jnp.zeros((CONV_ROWS, LANES), F32)
            for k in range(CONV_K):
                acc = acc + w[k:k + 1, s * LANES:(s + 1) * LANES] * win[k + shift:k + shift + CONV_ROWS, :]
            strips.append(acc)
        y = jnp.concatenate(strips, axis=1) + b_ref[...]
        y = _ln(y) * g_ref[...] + beta_ref[...]
        o_ref[pl.ds(t0, CONV_ROWS), :] = (y * _sigmoid(y)).astype(BF16)
        return carry

    lax.fori_loop(0, seq // CONV_ROWS, chunk, 0)


def _conv(u, dw_w, dw_b, ln_g, ln_b, *, seq, nseq, row_block0):
    width = dw_w.shape[1]
    wpad = jnp.zeros((32, width), F32).at[:CONV_K].set(dw_w)
    const2 = lambda b: (0, 0)
    return pl.pallas_call(
        functools.partial(_conv_kernel, seq=seq, width=width),
        grid=(nseq,),
        in_specs=[pl.BlockSpec((seq, width), lambda b: (row_block0 + b, 0)),
                  pl.BlockSpec((seq, width), lambda b: (row_block0 + b, 1)),
                  pl.BlockSpec((32, width), const2),
                  pl.BlockSpec((1, width), const2), pl.BlockSpec((1, width), const2),
                  pl.BlockSpec((1, width), const2)],
        out_specs=pl.BlockSpec((seq, width), lambda b: (b, 0)),
        out_shape=jax.ShapeDtypeStruct((nseq * seq, width), BF16),
        scratch_shapes=[pltpu.VMEM((seq + 2 * CONV_PAD, width), F32)],
        compiler_params=_params("parallel"),
        name="conv",
    )(u, u, wpad, dw_b.reshape(1, width), ln_g.reshape(1, width), ln_b.reshape(1, width))


def _rope_kernel(q_ref, k_ref, cos_ref, sin_ref, qo_ref, ko_ref, *, scale):
    cos = cos_ref[...]
    sin = sin_ref[...]
    width = cos.shape[1]
    lane = lax.broadcasted_iota(I32, cos.shape, 1)
    first = (lane & (DIFF_D // 2 - 1)) < (DIFF_D // 4)

    def rot(x):
        partner = jnp.where(first, pltpu.roll(x, width - DIFF_D // 4, 1), pltpu.roll(x, DIFF_D // 4, 1))
        return x * cos + partner * sin

    qo_ref[...] = (rot(q_ref[...].astype(F32)) * scale).astype(BF16)
    ko_ref[...] = rot(k_ref[...].astype(F32)).astype(BF16)


def _rope(u, cos, sin, *, nlat, seq, tm, qcol, kcol):
    width = cos.shape[1]
    per_seq = seq // tm
    spec_tab = pl.BlockSpec((tm, width), lambda i: (i % per_seq, 0))
    out = jax.ShapeDtypeStruct((nlat, width), BF16)
    return pl.pallas_call(
        functools.partial(_rope_kernel, scale=DIFF_D ** -0.5 * math.log2(math.e)),
        grid=(nlat // tm,),
        in_specs=[pl.BlockSpec((tm, width), lambda i: (i, qcol)),
                  pl.BlockSpec((tm, width), lambda i: (i, kcol)),
                  spec_tab, spec_tab],
        out_specs=[pl.BlockSpec((tm, width), lambda i: (i, 0))] * 2,
        out_shape=[out, out],
        compiler_params=_params("parallel"),
        name="rope",
    )(u, u, cos, sin)


def _rope_tables(seq, n_maps):
    axis_dim = DIFF_D // 2
    inv_freq = ROPE_BASE ** (-jnp.arange(0, axis_dim, 2, dtype=F32) / axis_dim)
    t = jnp.arange(seq, dtype=I32)
    ang_r = (t // GRID_W).astype(F32)[:, None] * inv_freq[None, :]
    ang_c = (t % GRID_W).astype(F32)[:, None] * inv_freq[None, :]
    cos = jnp.concatenate([jnp.cos(ang_r)] * 2 + [jnp.cos(ang_c)] * 2, axis=1)
    sin = jnp.concatenate([-jnp.sin(ang_r), jnp.sin(ang_r), -jnp.sin(ang_c), jnp.sin(ang_c)], axis=1)
    return jnp.tile(cos, (1, n_maps)), jnp.tile(sin, (1, n_maps))


def _lambda(lam_ref, lam_init):
    lv = lam_ref[...]
    return (jnp.exp(jnp.sum(lv[0:1] * lv[1:2], axis=-1, keepdims=True))
            - jnp.exp(jnp.sum(lv[2:3] * lv[3:4], axis=-1, keepdims=True)) + lam_init)


def _softmax_parts(score_parts):
    m = score_parts[0].max(axis=-1, keepdims=True)
    for s in score_parts[1:]:
        m = jnp.maximum(m, s.max(axis=-1, keepdims=True))
    es = [jnp.exp(s - m) for s in score_parts]
    tot = es[0].sum(axis=-1, keepdims=True)
    for e in es[1:]:
        tot = tot + e.sum(axis=-1, keepdims=True)
    return es, 1.0 / tot


def _diff_head(q1, q2, keys1, keys2, vals, lam, g, lam_init):
    e1, inv1 = _softmax_parts([_dot_nt(q1, k) for k in keys1])
    e2, inv2 = _softmax_parts([_dot_nt(q2, k) for k in keys2])
    o = None
    for a, b, v in zip(e1, e2, vals):
        p = (a * inv1 - lam * (b * inv2)).astype(BF16)
        o = _dot(p, v) if o is None else o + _dot(p, v)
    o = o * lax.rsqrt(jnp.mean(o * o, axis=-1, keepdims=True) + LN_EPS)
    return o * g * (1.0 - lam_init)


def _diff_kernel(q_ref, k_ref, v_ref, kc_ref, vc_ref, lam_ref, g_ref, o_ref, kall, vaug, *, lam_init, heads, n_ctx):
    hd = 2 * DIFF_D

    @pl.when(pl.program_id(1) == 0)
    def _():
        kall[0:n_ctx, :] = kc_ref[...]
        kall[n_ctx:, :] = k_ref[...]
        ones = jnp.ones((kall.shape[0], hd), BF16)
        for h in range(heads):
            vaug[h, 0:n_ctx, 0:hd] = vc_ref[:, h * hd:(h + 1) * hd]
            vaug[h, n_ctx:, 0:hd] = v_ref[:, h * hd:(h + 1) * hd]
            vaug[h, :, hd:2 * hd] = ones

    lam = _lambda(lam_ref, lam_init)
    g = g_ref[...]
    for h in range(heads):
        c0 = h * hd

        def attend(lo):
            s = _dot_nt(q_ref[:, lo:lo + DIFF_D], kall[:, lo:lo + DIFF_D])
            e = jnp.exp2(s - s.max(axis=-1, keepdims=True)).astype(BF16)
            r = _dot(e, vaug[h])
            return r[:, :hd] * (1.0 / r[:, hd:hd + 1])

        o = attend(c0) - lam * attend(c0 + DIFF_D)
        o = o * lax.rsqrt(jnp.mean(o * o, axis=-1, keepdims=True) + LN_EPS)
        o_ref[:, c0:c0 + hd] = (o * g * (1.0 - lam_init)).astype(BF16)


def _diff_attn(q_rot, k_rot, u, lam_vecs, subln_g, *, batch, seq, n_ctx, tq, lam_init, kcol, vcol):
    width = q_rot.shape[1]
    heads = width // (2 * DIFF_D)
    nq = seq // tq
    ctx_blk0 = batch * seq // n_ctx
    return pl.pallas_call(
        functools.partial(_diff_kernel, lam_init=lam_init, heads=heads, n_ctx=n_ctx),
        grid=(batch, nq),
        scratch_shapes=[pltpu.VMEM((seq + n_ctx, width), BF16),
                        pltpu.VMEM((heads, seq + n_ctx, 4 * DIFF_D), BF16)],
        in_specs=[pl.BlockSpec((tq, width), lambda b, i: (b * nq + i, 0)),
                  pl.BlockSpec((seq, width), lambda b, i: (b, 0)),
                  pl.BlockSpec((seq, width), lambda b, i: (b, vcol)),
                  pl.BlockSpec((n_ctx, width), lambda b, i: (ctx_blk0 + b, kcol)),
                  pl.BlockSpec((n_ctx, width), lambda b, i: (ctx_blk0 + b, vcol)),
                  pl.BlockSpec((4, DIFF_D), lambda b, i: (0, 0)),
                  pl.BlockSpec((1, 2 * DIFF_D), lambda b, i: (0, 0))],
        out_specs=pl.BlockSpec((tq, width), lambda b, i: (b * nq + i, 0)),
        out_shape=jax.ShapeDtypeStruct((batch * seq, width), BF16),
        compiler_params=_params("parallel", "arbitrary"),
        name="diff_attn",
    )(q_rot, k_rot, u, u, u, lam_vecs, subln_g.reshape(1, 2 * DIFF_D))


def _ctx_attn_kernel(bq_ref, bk_ref, bv_ref, cq_ref, ck_ref, cv_ref, lam_ref, g_ref, yb_ref, yc_ref, *, lam_init):
    lam = _lambda(lam_ref, lam_init)
    g = g_ref[...]
    width = bq_ref.shape[1]
    hd = 2 * DIFF_D
    scale = jnp.asarray(DIFF_D ** -0.5, BF16)
    for h in range(width // hd):
        c0 = h * hd
        o = _diff_head(bq_ref[:, c0:c0 + DIFF_D] * scale, bq_ref[:, c0 + DIFF_D:c0 + hd] * scale,
                       [bk_ref[:, c0:c0 + DIFF_D]], [bk_ref[:, c0 + DIFF_D:c0 + hd]],
                       [bv_ref[:, c0:c0 + hd]], lam, g, lam_init)
        yb_ref[:, c0:c0 + hd] = o.astype(BF16)
    scale_c = jnp.asarray(HEAD_DIM ** -0.5, BF16)
    for h in range(width // HEAD_DIM):
        c0 = h * HEAD_DIM
        es, inv = _softmax_parts([_dot_nt(cq_ref[:, c0:c0 + HEAD_DIM] * scale_c, ck_ref[:, c0:c0 + HEAD_DIM])])
        o = _dot((es[0] * inv).astype(BF16), cv_ref[:, c0:c0 + HEAD_DIM])
        yc_ref[:, c0:c0 + HEAD_DIM] = o.astype(BF16)


def _ctx_attn(u, lam_vecs, subln_g, *, batch, seq, n_ctx, width, lam_init, cols):
    blk0 = batch * seq // n_ctx
    u_specs = [pl.BlockSpec((n_ctx, width), functools.partial(lambda b, col: (blk0 + b, col), col=col))
               for col in cols]
    out_spec = pl.BlockSpec((n_ctx, width), lambda b: (b, 0))
    out = jax.ShapeDtypeStruct((batch * n_ctx, width), BF16)
    return pl.pallas_call(
        functools.partial(_ctx_attn_kernel, lam_init=lam_init),
        grid=(batch,),
        in_specs=u_specs + [pl.BlockSpec((4, DIFF_D), lambda b: (0, 0)),
                            pl.BlockSpec((1, 2 * DIFF_D), lambda b: (0, 0))],
        out_specs=[out_spec, out_spec],
        out_shape=[out, out],
        compiler_params=_params("parallel"),
        name="ctx_attn",
    )(u, u, u, u, u, u, lam_vecs, subln_g.reshape(1, 2 * DIFF_D))


def _nat_kernel(q_ref, k_ref, v_ref, kc_ref, vc_ref, bias_ref, o_ref, *, rows):
    j = pl.program_id(1)
    span = NAT_SPAN_ROWS * GRID_W
    start_row = jnp.clip(j * NAT_QROWS - NAT_KR // 2, 0, rows - NAT_SPAN_ROWS)
    start = pl.multiple_of(start_row * GRID_W, NAT_QROWS * GRID_W)
    scale = jnp.asarray(HEAD_DIM ** -0.5, BF16)
    for h in range(q_ref.shape[1] // HEAD_DIM):
        c0 = h * HEAD_DIM
        q = q_ref[:, c0:c0 + HEAD_DIM] * scale
        s_loc = _dot_nt(q, k_ref[pl.ds(start, span), c0:c0 + HEAD_DIM]) + bias_ref[0, h]
        s_ctx = _dot_nt(q, kc_ref[:, c0:c0 + HEAD_DIM])
        (e_loc, e_ctx), inv = _softmax_parts([s_loc, s_ctx])
        o = (_dot(e_loc.astype(BF16), v_ref[pl.ds(start, span), c0:c0 + HEAD_DIM])
             + _dot(e_ctx.astype(BF16), vc_ref[:, c0:c0 + HEAD_DIM]))
        o_ref[:, c0:c0 + HEAD_DIM] = (o * inv).astype(BF16)


def _nat_bias_tables(rpb, rows):
    n_groups = rows // NAT_QROWS
    tabs = []
    for j in (0, 1, n_groups - 1):
        start = int(np.clip(j * NAT_QROWS - NAT_KR // 2, 0, rows - NAT_SPAN_ROWS))
        r = j * NAT_QROWS + np.arange(NAT_QROWS)
        kr = start + np.arange(NAT_SPAN_ROWS)
        rs = np.clip(r - NAT_KR // 2, 0, rows - NAT_KR)
        row_ok = (kr[None, :] >= rs[:, None]) & (kr[None, :] < rs[:, None] + NAT_KR)
        d_row = np.clip(kr[None, :] - r[:, None] + NAT_KR - 1, 0, 2 * NAT_KR - 2)
        c = np.arange(GRID_W)
        cs = np.clip(c - NAT_KC // 2, 0, GRID_W - NAT_KC)
        col_ok = (c[None, :] >= cs[:, None]) & (c[None, :] < cs[:, None] + NAT_KC)
        d_col = np.clip(c[None, :] - c[:, None] + NAT_KC - 1, 0, 2 * NAT_KC - 2)
        pick = (d_col[:, :, None] == np.arange(2 * NAT_KC - 1)[None, None, :]).astype(np.float32)
        t = jnp.einsum("hqsd,cxd->hqcsx", rpb.astype(F32)[:, d_row], pick, precision=lax.Precision.HIGHEST)
        ok = row_ok[:, None, :, None] & col_ok[None, :, None, :]
        t = jnp.where(ok[None], t, NEG_BIG)
        tabs.append(t.reshape(rpb.shape[0], NAT_QROWS * GRID_W, NAT_SPAN_ROWS * GRID_W))
    return jnp.stack(tabs)


def _nat(u, bias, *, batch, seq, n_ctx, qcol, kcol, vcol):
    width = 8 * HEAD_DIM
    rows = seq // GRID_W
    n_groups = rows // NAT_QROWS
    tq = NAT_QROWS * GRID_W
    ctx_blk0 = batch * seq // n_ctx
    heads = bias.shape[1]

    def bias_idx(b, j):
        return (jnp.where(j == 0, 0, jnp.where(j == n_groups - 1, 2, 1)), 0, 0, 0)

    return pl.pallas_call(
        functools.partial(_nat_kernel, rows=rows),
        grid=(batch, n_groups),
        in_specs=[pl.BlockSpec((tq, width), lambda b, j: (b * n_groups + j, qcol)),
                  pl.BlockSpec((seq, width), lambda b, j: (b, kcol)),
                  pl.BlockSpec((seq, width), lambda b, j: (b, vcol)),
                  pl.BlockSpec((n_ctx, width), lambda b, j: (ctx_blk0 + b, kcol)),
                  pl.BlockSpec((n_ctx, width), lambda b, j: (ctx_blk0 + b, vcol)),
                  pl.BlockSpec((1, heads, tq, NAT_SPAN_ROWS * GRID_W), bias_idx)],
        out_specs=pl.BlockSpec((tq, width), lambda b, j: (b * n_groups + j, 0)),
        out_shape=jax.ShapeDtypeStruct((batch * seq, width), BF16),
        compiler_params=_params("parallel", "arbitrary"),
        name="nat",
    )(u, u, u, u, u, bias)


def _merge_kernel(ya_ref, yb_ref, yc_ref, ya_ctx_ref, yb_ctx_ref, yc_ctx_ref, ga_ref, gb_ref, gc_ref, x_ref, mod_ref,
                  wb_ref, wo_ref, lg_ref, lb_ref, x1_ref, tok_ref, *, d, alpha, lat_tiles):
    m = mod_ref[0]
    is_ctx = pl.program_id(0) >= lat_tiles
    ya = jnp.where(is_ctx, ya_ctx_ref[...], ya_ref[...])
    yb = jnp.where(is_ctx, yb_ctx_ref[...], yb_ref[...])
    yc = jnp.where(is_ctx, yc_ctx_ref[...], yc_ref[...])
    acc = _sigmoid(ga_ref[...].astype(F32)) * _dot(ya, wb_ref[0])
    acc = acc + _sigmoid(gb_ref[...].astype(F32)) * _dot(yb, wb_ref[1])
    acc = acc + _sigmoid(gc_ref[...].astype(F32)) * _dot(yc, wb_ref[2])
    y = _dot(acc.astype(BF16), wo_ref[...])
    x1 = _ln(alpha * x_ref[...] + m[:, 2 * d:3 * d] * y) * lg_ref[...] + lb_ref[...]
    x1_ref[...] = x1
    tok_ref[...] = _ln(x1) * (1.0 + m[:, 4 * d:5 * d]) + m[:, 3 * d:4 * d]


def _merge(y_lat, y_ctx, u, xa, mod3, wb_bf, wo_bf, ln_g, ln_b, mod_row, *, n_tok, tm, gcol0, alpha):
    d = xa.shape[1]
    half = y_lat[0].shape[1]
    lat_tiles = y_lat[0].shape[0] // tm
    ctx_tiles = y_ctx[0].shape[0] // tm
    tile = lambda i: (i, 0)
    const2 = lambda i: (0, 0)
    lat_tile = lambda i: (jnp.minimum(i, lat_tiles - 1), 0)
    ctx_tile = lambda i: (jnp.clip(i - lat_tiles, 0, ctx_tiles - 1), 0)
    out = jax.ShapeDtypeStruct((n_tok, d), F32)
    return pl.pallas_call(
        functools.partial(_merge_kernel, d=d, alpha=alpha, lat_tiles=lat_tiles),
        grid=(n_tok // tm,),
        in_specs=[pl.BlockSpec((tm, half), lat_tile)] * 3 + [pl.BlockSpec((tm, half), ctx_tile)] * 3
                 + [pl.BlockSpec((tm, d), functools.partial(lambda i, col: (i, col), col=gcol0 + k)) for k in range(3)]
                 + [pl.BlockSpec((tm, d), tile),
                    pl.BlockSpec((1, 1, 6 * d), lambda i: (mod_row(i, tm), 0, 0)),
                    pl.BlockSpec((3, half, d), lambda i: (0, 0, 0)),
                    pl.BlockSpec((d, d), const2),
                    pl.BlockSpec((1, d), const2), pl.BlockSpec((1, d), const2)],
        out_specs=[pl.BlockSpec((tm, d), tile)] * 2,
        out_shape=[out, out],
        compiler_params=_params("parallel"),
        name="merge",
    )(*y_lat, *y_ctx, u, u, u, xa, mod3, wb_bf, wo_bf, ln_g.reshape(1, d), ln_b.reshape(1, d))


def _router_kernel(tok_ref, w_ref, b_ref, idx_ref, wts_ref, rank_ref, cnt_ref):
    tm = tok_ref.shape[0]
    logits = _dot(tok_ref[...].astype(BF16), w_ref[...]) + b_ref[...]
    lane = lax.broadcasted_iota(I32, logits.shape, 1).astype(F32)
    work = logits
    idxs, vals = [], []
    for _ in range(TOP_K):
        m = work.max(axis=-1, keepdims=True)
        sel = jnp.min(jnp.where(work == m, lane, float(LANES)), axis=-1, keepdims=True)
        idxs.append(sel)
        vals.append(m)
        work = jnp.where(lane == sel, -jnp.inf, work)
    es = [jnp.exp(v - vals[0]) for v in vals]
    inv = 1.0 / (es[0] + es[1] + es[2] + es[3])

    onehot = jnp.zeros(logits.shape, F32)
    for sel in idxs:
        onehot = onehot + jnp.where(lane == sel, 1.0, 0.0)
    r_i = lax.broadcasted_iota(I32, (tm, tm), 0)
    c_i = lax.broadcasted_iota(I32, (tm, tm), 1)
    tri = jnp.where(c_i < r_i, 1.0, 0.0).astype(BF16)
    before = _dot(tri, onehot.astype(BF16))

    idx_out = jnp.zeros(logits.shape, F32)
    wts_out = jnp.zeros(logits.shape, F32)
    rank_out = jnp.zeros(logits.shape, F32)
    for k in range(TOP_K):
        rank_k = jnp.sum(jnp.where(lane == idxs[k], before, 0.0), axis=-1, keepdims=True)
        idx_out = jnp.where(lane == float(k), idxs[k], idx_out)
        wts_out = jnp.where(lane == float(k), es[k] * inv, wts_out)
        rank_out = jnp.where(lane == float(k), rank_k, rank_out)
    idx_ref[...] = idx_out.astype(I32)
    wts_ref[...] = wts_out
    rank_ref[...] = rank_out.astype(I32)
    cnt_ref[0] = jnp.sum(onehot, axis=0, keepdims=True)


def _router(tok, w_r, b_r, *, tm):
    t, d = tok.shape
    w_pad = jnp.zeros((d, LANES), BF16).at[:, :N_EXPERTS].set(w_r.astype(BF16))
    b_pad = jnp.full((1, LANES), NEG_BIG, F32).at[0, :N_EXPERTS].set(b_r)
    tile = lambda i: (i, 0)
    const2 = lambda i: (0, 0)
    return pl.pallas_call(
        _router_kernel,
        grid=(t // tm,),
        in_specs=[pl.BlockSpec((tm, d), tile), pl.BlockSpec((d, LANES), const2), pl.BlockSpec((1, LANES), const2)],
        out_specs=[pl.BlockSpec((tm, LANES), tile)] * 3 + [pl.BlockSpec((1, 1, LANES), lambda i: (i, 0, 0))],
        out_shape=[jax.ShapeDtypeStruct((t, LANES), I32), jax.ShapeDtypeStruct((t, LANES), F32),
                   jax.ShapeDtypeStruct((t, LANES), I32), jax.ShapeDtypeStruct((t // tm, 1, LANES), F32)],
        compiler_params=_params("parallel"),
        name="router",
    )(tok, w_pad, b_pad)


def _local_slots(idx_ref, rank_ref, loff_ref):
    idx = idx_ref[...]
    rank = rank_ref[...].astype(F32)
    loff = loff_ref[0].astype(F32)
    lane = lax.broadcasted_iota(I32, idx.shape, 1)
    slots = []
    for k in range(TOP_K):
        base = jnp.sum(jnp.where(lane == idx[:, k:k + 1], loff, 0.0), axis=-1, keepdims=True)
        slots.append(base + rank[:, k:k + 1])
    return slots


def _strip_copies(loff_s, goff_s, nch_s, blk, make, action):
    for e in range(N_EXPERTS):
        at = blk * N_EXPERTS + e

        def body(c, carry, at=at):
            local = pl.multiple_of(loff_s[at] + c * CHUNK, CHUNK)
            glob = pl.multiple_of(goff_s[at] + c * CHUNK, CHUNK)
            action(make(local, glob))
            return carry

        lax.fori_loop(0, nch_s[at], body, 0)


def _dispatch_kernel(loff_s, goff_s, nch_s, pend_ref, padded_ref, tok_ref, idx_ref, rank_ref, loff_ref, xs_ref,
                     loc_scr, zero_scr, sem, zsem, *, tile):
    i = pl.program_id(0)
    tb = tok_ref.shape[0]
    n_loc = loc_scr.shape[0]

    def zero_copy(e):
        off = pl.multiple_of(jnp.maximum(pend_ref[e] - tile, 0), tile)
        return pltpu.make_async_copy(zero_scr, xs_ref.at[pl.ds(off, tile), :], zsem)

    def tail_copy(j):
        return pltpu.make_async_copy(zero_scr, xs_ref.at[pl.ds(pl.multiple_of(j * tile, tile), tile), :], zsem)

    @pl.when(i == 0)
    def _():
        zero_scr[...] = jnp.zeros_like(zero_scr)
        n_used = pend_ref[N_EXPERTS - 1] // tile
        n_tiles = xs_ref.shape[0] // tile
        for e in range(N_EXPERTS):
            @pl.when(padded_ref[e] > 0)
            def _():
                zero_copy(e).start()

        def start_tail(j, carry):
            tail_copy(j).start()
            return carry

        def wait_tail(j, carry):
            tail_copy(j).wait()
            return carry

        lax.fori_loop(n_used, n_tiles, start_tail, 0)
        for e in range(N_EXPERTS):
            @pl.when(padded_ref[e] > 0)
            def _():
                zero_copy(e).wait()
        lax.fori_loop(n_used, n_tiles, wait_tail, 0)

    slots = _local_slots(idx_ref, rank_ref, loff_ref)
    lane = lax.broadcasted_iota(I32, (tb, LANES), 1)
    packed = jnp.full((tb, LANES), -1.0, F32)
    for k in range(TOP_K):
        packed = jnp.where(lane == k, slots[k], packed)
    slots_t = packed.T
    row = lax.broadcasted_iota(I32, (n_loc, tb), 0).astype(F32)
    pick = jnp.zeros((n_loc, tb), F32)
    for k in range(TOP_K):
        pick = jnp.where(row == slots_t[k:k + 1, :], 1.0, pick)
    loc_scr[...] = _dot(pick.astype(BF16), tok_ref[...].astype(BF16))

    def make(local, glob):
        return pltpu.make_async_copy(loc_scr.at[pl.ds(local, CHUNK), :], xs_ref.at[pl.ds(glob, CHUNK), :], sem)

    _strip_copies(loff_s, goff_s, nch_s, i, make, lambda cp: cp.start())
    _strip_copies(loff_s, goff_s, nch_s, i, make, lambda cp: cp.wait())


def _dispatch(tok, idx, rank, loff_vec, loff, goff, nch, pad_end, padded, *, tb, tile, n_slots):
    t, d = tok.shape
    n_loc = tb * TOP_K + N_EXPERTS * CHUNK
    tok_tile = lambda i, *_: (i, 0)
    grid_spec = pltpu.PrefetchScalarGridSpec(
        num_scalar_prefetch=5,
        grid=(t // tb,),
        in_specs=[pl.BlockSpec((tb, d), tok_tile), pl.BlockSpec((tb, LANES), tok_tile),
                  pl.BlockSpec((tb, LANES), tok_tile), pl.BlockSpec((1, 1, LANES), lambda i, *_: (i, 0, 0))],
        out_specs=pl.BlockSpec(memory_space=pl.ANY),
        scratch_shapes=[pltpu.VMEM((n_loc, d), F32), pltpu.VMEM((tile, d), F32),
                        pltpu.SemaphoreType.DMA(()), pltpu.SemaphoreType.DMA(())],
    )
    return pl.pallas_call(
        functools.partial(_dispatch_kernel, tile=tile),
        grid_spec=grid_spec,
        out_shape=jax.ShapeDtypeStruct((n_slots, d), F32),
        compiler_params=_params("arbitrary"),
        name="dispatch",
    )(loff, goff, nch, pad_end, padded, tok, idx, rank, loff_vec)


def _expert_kernel(te_ref, nu_ref, xs_ref, w1_ref, b1_ref, w2_ref, b2_ref, ys_ref, *, ff):
    del te_ref

    @pl.when(pl.program_id(0) >= nu_ref[0])
    def _():
        ys_ref[...] = jnp.zeros_like(ys_ref)

    @pl.when(pl.program_id(0) < nu_ref[0])
    def _():
        hh = _dot(xs_ref[...].astype(BF16), w1_ref[0]) + b1_ref[0]
        glu = jnp.minimum(hh[:, :ff], SWIGLU_LIMIT)
        lin = jnp.clip(hh[:, ff:], -SWIGLU_LIMIT, SWIGLU_LIMIT)
        act = glu * _sigmoid(SWIGLU_ALPHA * glu) * (lin + 1.0)
        ys_ref[...] = _dot(act.astype(BF16), w2_ref[0]) + b2_ref[0]


def _experts(xs, w1_bf, b1, w2_bf, b2, tile_expert, n_used, *, tile):
    n_slots, d = xs.shape
    ne, _, ff2 = w1_bf.shape
    ff = ff2 // 2
    row = lambda i, te, nu: (jnp.minimum(i, nu[0] - 1), 0)
    per_e = lambda i, te, nu: (te[i], 0, 0)
    grid_spec = pltpu.PrefetchScalarGridSpec(
        num_scalar_prefetch=2,
        grid=(n_slots // tile,),
        in_specs=[pl.BlockSpec((tile, d), row),
                  pl.BlockSpec((1, d, ff2), per_e), pl.BlockSpec((1, 1, ff2), per_e),
                  pl.BlockSpec((1, ff, d), per_e), pl.BlockSpec((1, 1, d), per_e)],
        out_specs=pl.BlockSpec((tile, d), lambda i, te, nu: (i, 0)),
    )
    return pl.pallas_call(
        functools.partial(_expert_kernel, ff=ff),
        grid_spec=grid_spec,
        out_shape=jax.ShapeDtypeStruct((n_slots, d), F32),
        compiler_params=_params("arbitrary"),
        name="experts",
    )(tile_expert, n_used, xs, w1_bf, b1.reshape(ne, 1, ff2), w2_bf, b2.reshape(ne, 1, d))


def _combine_kernel(loff_s, goff_s, nch_s, idx_ref, rank_ref, wts_ref, loff_ref, x1_ref, mod_ref, lg_ref, lb_ref,
                    ys_ref, o_ref, loc_scr, sem, *, d, alpha):
    i = pl.program_id(0)
    tb = x1_ref.shape[0]
    n_loc = loc_scr.shape[0]
    loc_scr[...] = jnp.zeros_like(loc_scr)

    def make(local, glob):
        return pltpu.make_async_copy(ys_ref.at[pl.ds(glob, CHUNK), :], loc_scr.at[pl.ds(local, CHUNK), :], sem)

    _strip_copies(loff_s, goff_s, nch_s, i, make, lambda cp: cp.start())
    _strip_copies(loff_s, goff_s, nch_s, i, make, lambda cp: cp.wait())

    slots = _local_slots(idx_ref, rank_ref, loff_ref)
    w = wts_ref[...]
    col = lax.broadcasted_iota(I32, (tb, n_loc), 1).astype(F32)
    wmat = jnp.zeros((tb, n_loc), F32)
    for k in range(TOP_K):
        wmat = jnp.where(col == slots[k], w[:, k:k + 1], wmat)
    y = _dot(wmat.astype(BF16), loc_scr[...].astype(BF16))
    m = mod_ref[0]
    o_ref[...] = _ln(alpha * x1_ref[...] + m[:, 5 * d:6 * d] * y) * lg_ref[...] + lb_ref[...]


def _combine(ys, idx, rank, wts, loff_vec, loff, goff, nch, x1, mod3, ln_g, ln_b, mod_row, *, tb, alpha):
    t, d = x1.shape
    n_loc = tb * TOP_K + N_EXPERTS * CHUNK
    tile = lambda i, *_: (i, 0)
    const2 = lambda i, *_: (0, 0)
    grid_spec = pltpu.PrefetchScalarGridSpec(
        num_scalar_prefetch=3,
        grid=(t // tb,),
        in_specs=[pl.BlockSpec((tb, LANES), tile), pl.BlockSpec((tb, LANES), tile), pl.BlockSpec((tb, LANES), tile),
                  pl.BlockSpec((1, 1, LANES), lambda i, *_: (i, 0, 0)),
                  pl.BlockSpec((tb, d), tile),
                  pl.BlockSpec((1, 1, 6 * d), lambda i, *_: (mod_row(i, tb), 0, 0)),
                  pl.BlockSpec((1, d), const2), pl.BlockSpec((1, d), const2),
                  pl.BlockSpec(memory_space=pl.ANY)],
        out_specs=pl.BlockSpec((tb, d), tile),
        scratch_shapes=[pltpu.VMEM((n_loc, d), F32), pltpu.SemaphoreType.DMA(())],
    )
    return pl.pallas_call(
        functools.partial(_combine_kernel, d=d, alpha=alpha),
        grid_spec=grid_spec,
        out_shape=jax.ShapeDtypeStruct((t, d), F32),
        compiler_params=_params("arbitrary"),
        name="combine",
    )(loff, goff, nch, idx, rank, wts, loff_vec, x1, mod3, ln_g.reshape(1, d), ln_b.reshape(1, d), ys)


def _moe(tok, x1, mod3, mod_row, router_w, router_b, w1_bf, b1, w2_bf, b2, ln_g, ln_b, *, alpha):
    t = tok.shape[0]
    tile, tb = MOE_TILE, MOE_BLOCK
    nblk = t // tb
    idx, wts, rank, counts = _router(tok, router_w, router_b, tm=tb)
    n = counts[:, 0, :N_EXPERTS].astype(I32)
    npad = (n + CHUNK - 1) // CHUNK * CHUNK
    loff = jnp.cumsum(npad, axis=1) - npad
    padded = (jnp.sum(npad, axis=0) + tile - 1) // tile * tile
    pad_end = jnp.cumsum(padded)
    goff = (pad_end - padded)[None, :] + jnp.cumsum(npad, axis=0) - npad
    nch = npad // CHUNK
    loff_vec = jnp.zeros((nblk, 1, LANES), I32).at[:, 0, :N_EXPERTS].set(loff)
    flat = lambda a: a.reshape(-1).astype(I32)
    n_tiles = -(-(t * TOP_K + (CHUNK - 1) * N_EXPERTS * nblk) // tile) + N_EXPERTS
    n_used = (pad_end[-1] // tile).astype(I32).reshape(1)
    tile_start = jnp.minimum(jnp.arange(n_tiles, dtype=I32), n_used[0] - 1) * tile
    tile_expert = jnp.minimum(jnp.sum(pad_end[None, :] <= tile_start[:, None], axis=1), N_EXPERTS - 1).astype(I32)
    xs = _dispatch(tok, idx, rank, loff_vec, flat(loff), flat(goff), flat(nch), flat(pad_end), flat(padded),
                   tb=tb, tile=tile, n_slots=n_tiles * tile)
    ys = _experts(xs, w1_bf, b1, w2_bf, b2, tile_expert, n_used, tile=tile)
    return _combine(ys, idx, rank, wts, loff_vec, flat(loff), flat(goff), flat(nch), x1, mod3, ln_g, ln_b, mod_row,
                    tb=tb, alpha=alpha)


def kernel(x, c, ctx, c_ctx, w_ada, b_ada, w_in, conv_dw_w, conv_dw_b, conv_ln_g, conv_ln_b, lam_q1, lam_k1, lam_q2,
           lam_k2, diff_subln_g, nat_rpb, w_branch, w_out, ln1_g, ln1_b, router_w, router_b, exp_w1, exp_b1, exp_w2,
           exp_b2, ln2_g, ln2_b):
    batch, seq, d = x.shape
    n_ctx = ctx.shape[1]
    depth = w_ada.shape[0]
    half = d // 2
    nlat, nctx = batch * seq, batch * n_ctx
    alpha = (2 * depth) ** 0.25
    assert seq % (GRID_W * NAT_QROWS) == 0 and seq // GRID_W >= NAT_SPAN_ROWS and seq % n_ctx == 0

    xa = jnp.concatenate([x.reshape(nlat, d), ctx.reshape(nctx, d)], axis=0)
    mod_rows = -(-(batch + 1) // SUBLANES) * SUBLANES
    cc = jnp.zeros((mod_rows, d), F32).at[:batch].set(c).at[batch].set(c_ctx)
    mod3 = _ada(cc, w_ada, b_ada).reshape(depth * mod_rows, 1, 6 * d)
    cos, sin = _rope_tables(seq, half // DIFF_D)

    for i in range(depth):
        with_ctx = i < depth - 1
        lam_init = 0.8 - 0.6 * math.exp(-0.3 * i)
        n_tok = nlat + nctx if with_ctx else nlat

        def mod_row(t, tm, i=i):
            return i * mod_rows + jnp.where(t * tm < nlat, (t * tm) // seq, batch)

        u = _inproj(xa, mod3, w_in[i].astype(BF16), mod_row, tm=math.gcd(1024, nctx), tn=1792)

        conv_p = (conv_dw_w[i], conv_dw_b[i], conv_ln_g[i], conv_ln_b[i])
        ya = _conv(u, *conv_p, seq=seq, nseq=batch, row_block0=0)
        q_rot, k_rot = _rope(u, cos, sin, nlat=nlat, seq=seq, tm=1024, qcol=2, kcol=3)
        lam_vecs = jnp.stack([lam_q1[i], lam_k1[i], lam_q2[i], lam_k2[i]]).astype(F32)
        yb = _diff_attn(q_rot, k_rot, u, lam_vecs, diff_subln_g[i], batch=batch, seq=seq, n_ctx=n_ctx, tq=256,
                        lam_init=lam_init, kcol=3, vcol=4)
        bias = _nat_bias_tables(nat_rpb[i], seq // GRID_W)
        yc = _nat(u, bias, batch=batch, seq=seq, n_ctx=n_ctx, qcol=5, kcol=6, vcol=7)
        y_lat = (ya, yb, yc)
        y_ctx = y_lat
        if with_ctx:
            ya_c = _conv(u, *conv_p, seq=n_ctx, nseq=batch, row_block0=nlat // n_ctx)
            yb_c, yc_c = _ctx_attn(u, lam_vecs, diff_subln_g[i], batch=batch, seq=seq, n_ctx=n_ctx, width=half,
                                   lam_init=lam_init, cols=(2, 3, 4, 5, 6, 7))
            y_ctx = (ya_c, yb_c, yc_c)

        x1, tok = _merge(y_lat, y_ctx, u, xa, mod3, w_branch[i].astype(BF16), w_out[i].astype(BF16), ln1_g[i],
                         ln1_b[i], mod_row, n_tok=n_tok, tm=math.gcd(512, nctx), gcol0=4, alpha=alpha)
        xa_new = _moe(tok, x1, mod3, mod_row, router_w[i], router_b[i], exp_w1[i].astype(BF16), exp_b1[i],
                      exp_w2[i].astype(BF16), exp_b2[i], ln2_g[i], ln2_b[i], alpha=alpha)
        xa = xa_new
    return xa[:nlat].reshape(batch, seq, d)
```

```python
import functools
import math

import numpy as np
import jax
import jax.numpy as jnp
from jax import lax
from jax.experimental import pallas as pl
from jax.experimental.pallas import tpu as pltpu

F32 = jnp.float32
BF16 = jnp.bfloat16
I32 = jnp.int32

GRID_W = 64
CONV_K = 31
DIFF_D = 64
HEAD_DIM = 64
NAT_KR = 8
NAT_KC = 16
N_EXPERTS = 32
TOP_K = 4
SWIGLU_LIMIT = 7.0
SWIGLU_ALPHA = 1.702
ROPE_BASE = 10000.0
LN_EPS = 1e-6
NEG_BIG = -1e30

LANES = 128
SUBLANES = 8
VMEM_LIMIT = 56 * 1024 * 1024

NAT_QROWS = 4
NAT_SPAN_ROWS = NAT_QROWS + NAT_KR
MOE_TILE = 512
MOE_BLOCK = 256
CHUNK = SUBLANES


def _sigmoid(x):
    return 1.0 / (1.0 + jnp.exp(-x))


def _ln(x):
    mu = jnp.mean(x, axis=-1, keepdims=True)
    xc = x - mu
    var = jnp.mean(xc * xc, axis=-1, keepdims=True)
    return xc * lax.rsqrt(var + LN_EPS)


def _dot(a, b):
    return jnp.dot(a, b, preferred_element_type=F32)


def _dot_nt(a, b):
    return lax.dot_general(a, b, (((1,), (1,)), ((), ())), preferred_element_type=F32)


def _params(*sem):
    return pltpu.CompilerParams(dimension_semantics=sem, vmem_limit_bytes=VMEM_LIMIT)


def _ada_kernel(cc_ref, w_ref, b_ref, o_ref):
    cc = cc_ref[...]
    s = cc * _sigmoid(cc)
    o_ref[0] = _dot(s.astype(BF16), w_ref[0].astype(BF16)) + b_ref[0]


def _ada(cc, w_ada, b_ada):
    depth, d, n = w_ada.shape
    rows = cc.shape[0]
    tn = 1536
    return pl.pallas_call(
        _ada_kernel,
        grid=(depth, n // tn),
        in_specs=[pl.BlockSpec((rows, d), lambda l, j: (0, 0)),
                  pl.BlockSpec((1, d, tn), lambda l, j: (l, 0, j)),
                  pl.BlockSpec((1, 1, tn), lambda l, j: (l, 0, j))],
        out_specs=pl.BlockSpec((1, rows, tn), lambda l, j: (l, 0, j)),
        out_shape=jax.ShapeDtypeStruct((depth, rows, n), F32),
        compiler_params=_params("parallel", "parallel"),
        name="ada",
    )(cc, w_ada, b_ada.reshape(depth, 1, n))


def _two_source_specs(parts, tm):
    a_tiles = parts[0].shape[0] // tm
    b_tiles = parts[1].shape[0] // tm
    d = parts[0].shape[1]
    spec_a = pl.BlockSpec((tm, d), lambda i, *_: (jnp.minimum(i, a_tiles - 1), 0))
    spec_b = pl.BlockSpec((tm, d), lambda i, *_: (jnp.clip(i - a_tiles, 0, b_tiles - 1), 0))
    return [spec_a, spec_b], a_tiles


def _inproj_kernel(xa_ref, xb_ref, mod_ref, w_ref, u_ref, h_scr, *, d, a_tiles):
    @pl.when(pl.program_id(1) == 0)
    def _():
        m = mod_ref[0]
        x = jnp.where(pl.program_id(0) >= a_tiles, xb_ref[...], xa_ref[...])
        h = _ln(x) * (1.0 + m[:, d:2 * d]) + m[:, 0:d]
        h_scr[...] = h.astype(BF16)

    u_ref[...] = _dot(h_scr[...], w_ref[...]).astype(BF16)


def _inproj(x_parts, mod3, w_bf, mod_row, *, n_rows, tm, tn, layer):
    d = x_parts[0].shape[1]
    n = w_bf.shape[2]
    x_specs, a_tiles = _two_source_specs(x_parts, tm)
    return pl.pallas_call(
        functools.partial(_inproj_kernel, d=d, a_tiles=a_tiles),
        grid=(n_rows // tm, n // tn),
        in_specs=x_specs + [pl.BlockSpec((1, 1, 6 * d), lambda i, j: (mod_row(i, tm), 0, 0)),
                            pl.BlockSpec((None, d, tn), lambda i, j: (layer, 0, j))],
        out_specs=pl.BlockSpec((tm, tn), lambda i, j: (i, j)),
        out_shape=jax.ShapeDtypeStruct((n_rows, n), BF16),
        scratch_shapes=[pltpu.VMEM((tm, d), BF16)],
        compiler_params=_params("parallel", "arbitrary"),
        name="inproj",
    )(*x_parts, mod3, w_bf)


CONV_PAD = 16
CONV_ROWS = 64


def _conv_kernel(val_ref, gate_ref, w_ref, b_ref, g_ref, beta_ref, o_ref, hp_scr, *, seq, width):
    zeros = jnp.zeros((CONV_PAD, width), F32)
    hp_scr[0:CONV_PAD, :] = zeros
    hp_scr[seq + CONV_PAD:seq + 2 * CONV_PAD, :] = zeros
    hp_scr[CONV_PAD:seq + CONV_PAD, :] = val_ref[...].astype(F32) * _sigmoid(gate_ref[...].astype(F32))
    w = w_ref[...]
    shift = CONV_PAD - CONV_K // 2

    def chunk(ci, carry):
        t0 = pl.multiple_of(ci * CONV_ROWS, CONV_ROWS)
        strips = []
        for s in range(width // LANES):
            win = hp_scr[pl.ds(t0, CONV_ROWS + 2 * CONV_PAD), s * LANES:(s + 1) * LANES]
            acc = jnp.zeros((CONV_ROWS, LANES), F32)
            for r in range(SUBLANES):
                shifted = win if r == 0 else pltpu.roll(win, win.shape[0] - r, 0)
                for base in range(0, 2 * CONV_PAD, SUBLANES):
                    k = base + r - shift
                    if 0 <= k < CONV_K:
                        acc = acc + w[k:k + 1, s * LANES:(s + 1) * LANES] * shifted[base:base + CONV_ROWS, :]
            strips.append(acc)
        y = jnp.concatenate(strips, axis=1) + b_ref[...]
        y = _ln(y) * g_ref[...] + beta_ref[...]
        o_ref[pl.ds(t0, CONV_ROWS), :] = (y * _sigmoid(y)).astype(BF16)
        return carry

    lax.fori_loop(0, seq // CONV_ROWS, chunk, 0)


def _conv(u, dw_w, dw_b, ln_g, ln_b, *, seq, nseq, row_block0):
    width = dw_w.shape[1]
    wpad = jnp.zeros((32, width), F32).at[:CONV_K].set(dw_w)
    const2 = lambda b: (0, 0)
    return pl.pallas_call(
        functools.partial(_conv_kernel, seq=seq, width=width),
        grid=(nseq,),
        in_specs=[pl.BlockSpec((seq, width), lambda b: (row_block0 + b, 0)),
                  pl.BlockSpec((seq, width), lambda b: (row_block0 + b, 1)),
                  pl.BlockSpec((32, width), const2),
                  pl.BlockSpec((1, width), const2), pl.BlockSpec((1, width), const2),
                  pl.BlockSpec((1, width), const2)],
        out_specs=pl.BlockSpec((seq, width), lambda b: (b, 0)),
        out_shape=jax.ShapeDtypeStruct((nseq * seq, width), BF16),
        scratch_shapes=[pltpu.VMEM((seq + 2 * CONV_PAD, width), F32)],
        compiler_params=_params("parallel"),
        name="conv",
    )(u, u, wpad, dw_b.reshape(1, width), ln_g.reshape(1, width), ln_b.reshape(1, width))


def _rope_kernel(q_ref, k_ref, cos_ref, sin_ref, qo_ref, ko_ref, *, scale):
    cos = cos_ref[...]
    sin = sin_ref[...]
    width = cos.shape[1]
    lane = lax.broadcasted_iota(I32, cos.shape, 1)
    first = (lane & (DIFF_D // 2 - 1)) < (DIFF_D // 4)

    def rot(x):
        partner = jnp.where(first, pltpu.roll(x, width - DIFF_D // 4, 1), pltpu.roll(x, DIFF_D // 4, 1))
        return x * cos + partner * sin

    qo_ref[...] = (rot(q_ref[...].astype(F32)) * scale).astype(BF16)
    ko_ref[...] = rot(k_ref[...].astype(F32)).astype(BF16)


def _rope(u, cos, sin, *, nlat, seq, tm, qcol, kcol):
    width = cos.shape[1]
    per_seq = seq // tm
    spec_tab = pl.BlockSpec((tm, width), lambda i: (i % per_seq, 0))
    out = jax.ShapeDtypeStruct((nlat, width), BF16)
    return pl.pallas_call(
        functools.partial(_rope_kernel, scale=DIFF_D ** -0.5 * math.log2(math.e)),
        grid=(nlat // tm,),
        in_specs=[pl.BlockSpec((tm, width), lambda i: (i, qcol)),
                  pl.BlockSpec((tm, width), lambda i: (i, kcol)),
                  spec_tab, spec_tab],
        out_specs=[pl.BlockSpec((tm, width), lambda i: (i, 0))] * 2,
        out_shape=[out, out],
        compiler_params=_params("parallel"),
        name="rope",
    )(u, u, cos, sin)


def _rope_tables(seq, n_maps):
    axis_dim = DIFF_D // 2
    inv_freq = ROPE_BASE ** (-jnp.arange(0, axis_dim, 2, dtype=F32) / axis_dim)
    t = jnp.arange(seq, dtype=I32)
    ang_r = (t // GRID_W).astype(F32)[:, None] * inv_freq[None, :]
    ang_c = (t % GRID_W).astype(F32)[:, None] * inv_freq[None, :]
    cos = jnp.concatenate([jnp.cos(ang_r)] * 2 + [jnp.cos(ang_c)] * 2, axis=1)
    sin = jnp.concatenate([-jnp.sin(ang_r), jnp.sin(ang_r), -jnp.sin(ang_c), jnp.sin(ang_c)], axis=1)
    return jnp.tile(cos, (1, n_maps)), jnp.tile(sin, (1, n_maps))


def _lambda(lam_ref, lam_init):
    lv = lam_ref[...]
    return (jnp.exp(jnp.sum(lv[0:1] * lv[1:2], axis=-1, keepdims=True))
            - jnp.exp(jnp.sum(lv[2:3] * lv[3:4], axis=-1, keepdims=True)) + lam_init)


def _softmax_parts(score_parts):
    m = score_parts[0].max(axis=-1, keepdims=True)
    for s in score_parts[1:]:
        m = jnp.maximum(m, s.max(axis=-1, keepdims=True))
    es = [jnp.exp(s - m) for s in score_parts]
    tot = es[0].sum(axis=-1, keepdims=True)
    for e in es[1:]:
        tot = tot + e.sum(axis=-1, keepdims=True)
    return es, 1.0 / tot


def _diff_head(q1, q2, keys1, keys2, vals, lam, g, lam_init):
    e1, inv1 = _softmax_parts([_dot_nt(q1, k) for k in keys1])
    e2, inv2 = _softmax_parts([_dot_nt(q2, k) for k in keys2])
    o = None
    for a, b, v in zip(e1, e2, vals):
        p = (a * inv1 - lam * (b * inv2)).astype(BF16)
        o = _dot(p, v) if o is None else o + _dot(p, v)
    o = o * lax.rsqrt(jnp.mean(o * o, axis=-1, keepdims=True) + LN_EPS)
    return o * g * (1.0 - lam_init)


def _diff_kernel(q_ref, k_ref, v_ref, kc_ref, vc_ref, lam_ref, g_ref, o_ref, kall, vaug, *, lam_init, heads, n_ctx):
    hd = 2 * DIFF_D

    @pl.when(pl.program_id(1) == 0)
    def _():
        kall[0:n_ctx, :] = kc_ref[...]
        kall[n_ctx:, :] = k_ref[...]
        ones = jnp.ones((kall.shape[0], hd), BF16)
        for h in range(heads):
            vaug[h, 0:n_ctx, 0:hd] = vc_ref[:, h * hd:(h + 1) * hd]
            vaug[h, n_ctx:, 0:hd] = v_ref[:, h * hd:(h + 1) * hd]
            vaug[h, :, hd:2 * hd] = ones

    lam = _lambda(lam_ref, lam_init)
    g = g_ref[...]
    for h in range(heads):
        c0 = h * hd

        def attend(lo):
            s = _dot_nt(q_ref[:, lo:lo + DIFF_D], kall[:, lo:lo + DIFF_D])
            e = jnp.exp2(s - s.max(axis=-1, keepdims=True)).astype(BF16)
            r = _dot(e, vaug[h])
            return r[:, :hd] * (1.0 / r[:, hd:hd + 1])

        o = attend(c0) - lam * attend(c0 + DIFF_D)
        o = o * lax.rsqrt(jnp.mean(o * o, axis=-1, keepdims=True) + LN_EPS)
        o_ref[:, c0:c0 + hd] = (o * g * (1.0 - lam_init)).astype(BF16)


def _diff_attn(q_rot, k_rot, u, lam_vecs, subln_g, *, batch, seq, n_ctx, tq, lam_init, kcol, vcol):
    width = q_rot.shape[1]
    heads = width // (2 * DIFF_D)
    nq = seq // tq
    ctx_blk0 = batch * seq // n_ctx
    return pl.pallas_call(
        functools.partial(_diff_kernel, lam_init=lam_init, heads=heads, n_ctx=n_ctx),
        grid=(batch, nq),
        scratch_shapes=[pltpu.VMEM((seq + n_ctx, width), BF16),
                        pltpu.VMEM((heads, seq + n_ctx, 4 * DIFF_D), BF16)],
        in_specs=[pl.BlockSpec((tq, width), lambda b, i: (b * nq + i, 0)),
                  pl.BlockSpec((seq, width), lambda b, i: (b, 0)),
                  pl.BlockSpec((seq, width), lambda b, i: (b, vcol)),
                  pl.BlockSpec((n_ctx, width), lambda b, i: (ctx_blk0 + b, kcol)),
                  pl.BlockSpec((n_ctx, width), lambda b, i: (ctx_blk0 + b, vcol)),
                  pl.BlockSpec((4, DIFF_D), lambda b, i: (0, 0)),
                  pl.BlockSpec((1, 2 * DIFF_D), lambda b, i: (0, 0))],
        out_specs=pl.BlockSpec((tq, width), lambda b, i: (b * nq + i, 0)),
        out_shape=jax.ShapeDtypeStruct((batch * seq, width), BF16),
        compiler_params=_params("parallel", "arbitrary"),
        name="diff_attn",
    )(q_rot, k_rot, u, u, u, lam_vecs, subln_g.reshape(1, 2 * DIFF_D))


def _ctx_attn_kernel(bq_ref, bk_ref, bv_ref, cq_ref, ck_ref, cv_ref, lam_ref, g_ref, yb_ref, yc_ref, *, lam_init):
    lam = _lambda(lam_ref, lam_init)
    g = g_ref[...]
    width = bq_ref.shape[1]
    hd = 2 * DIFF_D
    scale = jnp.asarray(DIFF_D ** -0.5, BF16)
    for h in range(width // hd):
        c0 = h * hd
        o = _diff_head(bq_ref[:, c0:c0 + DIFF_D] * scale, bq_ref[:, c0 + DIFF_D:c0 + hd] * scale,
                       [bk_ref[:, c0:c0 + DIFF_D]], [bk_ref[:, c0 + DIFF_D:c0 + hd]],
                       [bv_ref[:, c0:c0 + hd]], lam, g, lam_init)
        yb_ref[:, c0:c0 + hd] = o.astype(BF16)
    scale_c = jnp.asarray(HEAD_DIM ** -0.5, BF16)
    for h in range(width // HEAD_DIM):
        c0 = h * HEAD_DIM
        es, inv = _softmax_parts([_dot_nt(cq_ref[:, c0:c0 + HEAD_DIM] * scale_c, ck_ref[:, c0:c0 + HEAD_DIM])])
        o = _dot((es[0] * inv).astype(BF16), cv_ref[:, c0:c0 + HEAD_DIM])
        yc_ref[:, c0:c0 + HEAD_DIM] = o.astype(BF16)


def _ctx_attn(u, lam_vecs, subln_g, *, batch, seq, n_ctx, width, lam_init, cols):
    blk0 = batch * seq // n_ctx
    u_specs = [pl.BlockSpec((n_ctx, width), functools.partial(lambda b, col: (blk0 + b, col), col=col))
               for col in cols]
    out_spec = pl.BlockSpec((n_ctx, width), lambda b: (b, 0))
    out = jax.ShapeDtypeStruct((batch * n_ctx, width), BF16)
    return pl.pallas_call(
        functools.partial(_ctx_attn_kernel, lam_init=lam_init),
        grid=(batch,),
        in_specs=u_specs + [pl.BlockSpec((4, DIFF_D), lambda b: (0, 0)),
                            pl.BlockSpec((1, 2 * DIFF_D), lambda b: (0, 0))],
        out_specs=[out_spec, out_spec],
        out_shape=[out, out],
        compiler_params=_params("parallel"),
        name="ctx_attn",
    )(u, u, u, u, u, u, lam_vecs, subln_g.reshape(1, 2 * DIFF_D))


def _nat_kernel(q_ref, k_ref, v_ref, kc_ref, vc_ref, bias_ref, o_ref, *, rows):
    j = pl.program_id(1)
    span = NAT_SPAN_ROWS * GRID_W
    start_row = jnp.clip(j * NAT_QROWS - NAT_KR // 2, 0, rows - NAT_SPAN_ROWS)
    start = pl.multiple_of(start_row * GRID_W, NAT_QROWS * GRID_W)
    scale = jnp.asarray(HEAD_DIM ** -0.5, BF16)
    for h in range(q_ref.shape[1] // HEAD_DIM):
        c0 = h * HEAD_DIM
        q = q_ref[:, c0:c0 + HEAD_DIM] * scale
        s_loc = _dot_nt(q, k_ref[pl.ds(start, span), c0:c0 + HEAD_DIM]) + bias_ref[0, h]
        s_ctx = _dot_nt(q, kc_ref[:, c0:c0 + HEAD_DIM])
        (e_loc, e_ctx), inv = _softmax_parts([s_loc, s_ctx])
        o = (_dot(e_loc.astype(BF16), v_ref[pl.ds(start, span), c0:c0 + HEAD_DIM])
             + _dot(e_ctx.astype(BF16), vc_ref[:, c0:c0 + HEAD_DIM]))
        o_ref[:, c0:c0 + HEAD_DIM] = (o * inv).astype(BF16)


def _nat_bias_tables(rpb, rows):
    n_groups = rows // NAT_QROWS
    tabs = []
    for j in (0, 1, n_groups - 1):
        start = int(np.clip(j * NAT_QROWS - NAT_KR // 2, 0, rows - NAT_SPAN_ROWS))
        r = j * NAT_QROWS + np.arange(NAT_QROWS)
        kr = start + np.arange(NAT_SPAN_ROWS)
        rs = np.clip(r - NAT_KR // 2, 0, rows - NAT_KR)
        row_ok = (kr[None, :] >= rs[:, None]) & (kr[None, :] < rs[:, None] + NAT_KR)
        d_row = np.clip(kr[None, :] - r[:, None] + NAT_KR - 1, 0, 2 * NAT_KR - 2)
        c = np.arange(GRID_W)
        cs = np.clip(c - NAT_KC // 2, 0, GRID_W - NAT_KC)
        col_ok = (c[None, :] >= cs[:, None]) & (c[None, :] < cs[:, None] + NAT_KC)
        d_col = np.clip(c[None, :] - c[:, None] + NAT_KC - 1, 0, 2 * NAT_KC - 2)
        pick = (d_col[:, :, None] == np.arange(2 * NAT_KC - 1)[None, None, :]).astype(np.float32)
        t = jnp.einsum("hqsd,cxd->hqcsx", rpb.astype(F32)[:, d_row], pick, precision=lax.Precision.HIGHEST)
        ok = row_ok[:, None, :, None] & col_ok[None, :, None, :]
        t = jnp.where(ok[None], t, NEG_BIG)
        tabs.append(t.reshape(rpb.shape[0], NAT_QROWS * GRID_W, NAT_SPAN_ROWS * GRID_W))
    return jnp.stack(tabs)


def _nat(u, bias, *, batch, seq, n_ctx, qcol, kcol, vcol):
    width = 8 * HEAD_DIM
    rows = seq // GRID_W
    n_groups = rows // NAT_QROWS
    tq = NAT_QROWS * GRID_W
    ctx_blk0 = batch * seq // n_ctx
    heads = bias.shape[1]

    def bias_idx(b, j):
        return (jnp.where(j == 0, 0, jnp.where(j == n_groups - 1, 2, 1)), 0, 0, 0)

    return pl.pallas_call(
        functools.partial(_nat_kernel, rows=rows),
        grid=(batch, n_groups),
        in_specs=[pl.BlockSpec((tq, width), lambda b, j: (b * n_groups + j, qcol)),
                  pl.BlockSpec((seq, width), lambda b, j: (b, kcol)),
                  pl.BlockSpec((seq, width), lambda b, j: (b, vcol)),
                  pl.BlockSpec((n_ctx, width), lambda b, j: (ctx_blk0 + b, kcol)),
                  pl.BlockSpec((n_ctx, width), lambda b, j: (ctx_blk0 + b, vcol)),
                  pl.BlockSpec((1, heads, tq, NAT_SPAN_ROWS * GRID_W), bias_idx)],
        out_specs=pl.BlockSpec((tq, width), lambda b, j: (b * n_groups + j, 0)),
        out_shape=jax.ShapeDtypeStruct((batch * seq, width), BF16),
        compiler_params=_params("parallel", "arbitrary"),
        name="nat",
    )(u, u, u, u, u, bias)


def _merge_kernel(ya_ref, yb_ref, yc_ref, ya_ctx_ref, yb_ctx_ref, yc_ctx_ref, ga_ref, gb_ref, gc_ref, xa_ref, xb_ref,
                  mod_ref, wb_ref, wo_ref, lg_ref, lb_ref, x1_ref, tok_ref, *, d, alpha, lat_tiles, a_tiles):
    m = mod_ref[0]
    is_ctx = pl.program_id(0) >= lat_tiles
    x = jnp.where(pl.program_id(0) >= a_tiles, xb_ref[...], xa_ref[...])
    ya = jnp.where(is_ctx, ya_ctx_ref[...], ya_ref[...])
    yb = jnp.where(is_ctx, yb_ctx_ref[...], yb_ref[...])
    yc = jnp.where(is_ctx, yc_ctx_ref[...], yc_ref[...])
    acc = _sigmoid(ga_ref[...].astype(F32)) * _dot(ya, wb_ref[0])
    acc = acc + _sigmoid(gb_ref[...].astype(F32)) * _dot(yb, wb_ref[1])
    acc = acc + _sigmoid(gc_ref[...].astype(F32)) * _dot(yc, wb_ref[2])
    y = _dot(acc.astype(BF16), wo_ref[...])
    x1 = _ln(alpha * x + m[:, 2 * d:3 * d] * y) * lg_ref[...] + lb_ref[...]
    x1_ref[...] = x1
    tok_ref[...] = _ln(x1) * (1.0 + m[:, 4 * d:5 * d]) + m[:, 3 * d:4 * d]


def _merge(y_lat, y_ctx, u, x_parts, mod3, wb_bf, wo_bf, ln_g, ln_b, mod_row, *, n_tok, tm, gcol0, alpha, layer):
    d = x_parts[0].shape[1]
    x_specs, a_tiles = _two_source_specs(x_parts, tm)
    half = y_lat[0].shape[1]
    lat_tiles = y_lat[0].shape[0] // tm
    ctx_tiles = y_ctx[0].shape[0] // tm
    tile = lambda i: (i, 0)
    const2 = lambda i: (0, 0)
    lat_tile = lambda i: (jnp.minimum(i, lat_tiles - 1), 0)
    ctx_tile = lambda i: (jnp.clip(i - lat_tiles, 0, ctx_tiles - 1), 0)
    out = jax.ShapeDtypeStruct((n_tok, d), F32)
    return pl.pallas_call(
        functools.partial(_merge_kernel, d=d, alpha=alpha, lat_tiles=lat_tiles, a_tiles=a_tiles),
        grid=(n_tok // tm,),
        in_specs=[pl.BlockSpec((tm, half), lat_tile)] * 3 + [pl.BlockSpec((tm, half), ctx_tile)] * 3
                 + [pl.BlockSpec((tm, d), functools.partial(lambda i, col: (i, col), col=gcol0 + k)) for k in range(3)]
                 + x_specs
                 + [pl.BlockSpec((1, 1, 6 * d), lambda i: (mod_row(i, tm), 0, 0)),
                    pl.BlockSpec((None, 3, half, d), lambda i: (layer, 0, 0, 0)),
                    pl.BlockSpec((None, d, d), lambda i: (layer, 0, 0)),
                    pl.BlockSpec((1, d), const2), pl.BlockSpec((1, d), const2)],
        out_specs=[pl.BlockSpec((tm, d), tile)] * 2,
        out_shape=[out, out],
        compiler_params=_params("parallel"),
        name="merge",
    )(*y_lat, *y_ctx, u, u, u, *x_parts, mod3, wb_bf, wo_bf, ln_g.reshape(1, d), ln_b.reshape(1, d))


def _router_kernel(tok_ref, w_ref, b_ref, idx_ref, wts_ref, rank_ref, cnt_ref):
    tm = tok_ref.shape[0]
    logits = _dot(tok_ref[...].astype(BF16), w_ref[...]) + b_ref[...]
    lane = lax.broadcasted_iota(I32, logits.shape, 1).astype(F32)
    work = logits
    idxs, vals = [], []
    for _ in range(TOP_K):
        m = work.max(axis=-1, keepdims=True)
        sel = jnp.min(jnp.where(work == m, lane, float(LANES)), axis=-1, keepdims=True)
        idxs.append(sel)
        vals.append(m)
        work = jnp.where(lane == sel, -jnp.inf, work)
    es = [jnp.exp(v - vals[0]) for v in vals]
    inv = 1.0 / (es[0] + es[1] + es[2] + es[3])

    onehot = jnp.zeros(logits.shape, F32)
    for sel in idxs:
        onehot = onehot + jnp.where(lane == sel, 1.0, 0.0)
    r_i = lax.broadcasted_iota(I32, (tm, tm), 0)
    c_i = lax.broadcasted_iota(I32, (tm, tm), 1)
    n_sub = cnt_ref.shape[0]
    tb = tm // n_sub
    log_tb = tb.bit_length() - 1
    assert tb == 1 << log_tb
    same_block = (c_i >> log_tb) == (r_i >> log_tb)
    tri = jnp.where((c_i < r_i) & same_block, 1.0, 0.0).astype(BF16)
    before = _dot(tri, onehot.astype(BF16))

    idx_out = jnp.zeros(logits.shape, F32)
    wts_out = jnp.zeros(logits.shape, F32)
    rank_out = jnp.zeros(logits.shape, F32)
    for k in range(TOP_K):
        rank_k = jnp.sum(jnp.where(lane == idxs[k], before, 0.0), axis=-1, keepdims=True)
        idx_out = jnp.where(lane == float(k), idxs[k], idx_out)
        wts_out = jnp.where(lane == float(k), es[k] * inv, wts_out)
        rank_out = jnp.where(lane == float(k), rank_k, rank_out)
    idx_ref[...] = idx_out.astype(I32)
    wts_ref[...] = wts_out
    rank_ref[...] = rank_out.astype(I32)
    for sub in range(n_sub):
        cnt_ref[sub] = jnp.sum(onehot[sub * tb:(sub + 1) * tb], axis=0, keepdims=True)


def _router(tok, w_r, b_r, *, tb, tm):
    t, d = tok.shape
    n_sub = tm // tb
    w_pad = jnp.zeros((d, LANES), BF16).at[:, :N_EXPERTS].set(w_r.astype(BF16))
    b_pad = jnp.full((1, LANES), NEG_BIG, F32).at[0, :N_EXPERTS].set(b_r)
    tile = lambda i: (i, 0)
    const2 = lambda i: (0, 0)
    return pl.pallas_call(
        _router_kernel,
        grid=(t // tm,),
        in_specs=[pl.BlockSpec((tm, d), tile), pl.BlockSpec((d, LANES), const2), pl.BlockSpec((1, LANES), const2)],
        out_specs=[pl.BlockSpec((tm, LANES), tile)] * 3 + [pl.BlockSpec((n_sub, 1, LANES), lambda i: (i, 0, 0))],
        out_shape=[jax.ShapeDtypeStruct((t, LANES), I32), jax.ShapeDtypeStruct((t, LANES), F32),
                   jax.ShapeDtypeStruct((t, LANES), I32), jax.ShapeDtypeStruct((t // tb, 1, LANES), F32)],
        compiler_params=_params("parallel"),
        name="router",
    )(tok, w_pad, b_pad)


def _local_slots(idx_ref, rank_ref, loff_ref):
    idx = idx_ref[...]
    rank = rank_ref[...].astype(F32)
    loff = loff_ref[0].astype(F32)
    lane = lax.broadcasted_iota(I32, idx.shape, 1)
    slots = []
    for k in range(TOP_K):
        base = jnp.sum(jnp.where(lane == idx[:, k:k + 1], loff, 0.0), axis=-1, keepdims=True)
        slots.append(base + rank[:, k:k + 1])
    return slots


def _strip_copies(loff_s, goff_s, nch_s, blk, make, action):
    for e in range(N_EXPERTS):
        at = blk * N_EXPERTS + e

        def body(c, carry, at=at):
            local = pl.multiple_of(loff_s[at] + c * CHUNK, CHUNK)
            glob = pl.multiple_of(goff_s[at] + c * CHUNK, CHUNK)
            action(make(local, glob))
            return carry

        lax.fori_loop(0, nch_s[at], body, 0)


def _dispatch_kernel(loff_s, goff_s, nch_s, pend_ref, padded_ref, tok_ref, idx_ref, rank_ref, loff_ref, xs_ref,
                     loc_scr, zero_scr, sem, zsem, *, tile):
    i = pl.program_id(0)
    tb = tok_ref.shape[0]
    n_loc = loc_scr.shape[0]

    def zero_copy(e):
        off = pl.multiple_of(jnp.maximum(pend_ref[e] - tile, 0), tile)
        return pltpu.make_async_copy(zero_scr, xs_ref.at[pl.ds(off, tile), :], zsem)

    def tail_copy(j):
        return pltpu.make_async_copy(zero_scr, xs_ref.at[pl.ds(pl.multiple_of(j * tile, tile), tile), :], zsem)

    @pl.when(i == 0)
    def _():
        zero_scr[...] = jnp.zeros_like(zero_scr)
        n_used = pend_ref[N_EXPERTS - 1] // tile
        n_tiles = xs_ref.shape[0] // tile
        for e in range(N_EXPERTS):
            @pl.when(padded_ref[e] > 0)
            def _():
                zero_copy(e).start()

        def start_tail(j, carry):
            tail_copy(j).start()
            return carry

        def wait_tail(j, carry):
            tail_copy(j).wait()
            return carry

        lax.fori_loop(n_used, n_tiles, start_tail, 0)
        for e in range(N_EXPERTS):
            @pl.when(padded_ref[e] > 0)
            def _():
                zero_copy(e).wait()
        lax.fori_loop(n_used, n_tiles, wait_tail, 0)

    slots = _local_slots(idx_ref, rank_ref, loff_ref)
    lane = lax.broadcasted_iota(I32, (tb, LANES), 1)
    packed = jnp.full((tb, LANES), -1.0, F32)
    for k in range(TOP_K):
        packed = jnp.where(lane == k, slots[k], packed)
    slots_t = packed.T
    row = lax.broadcasted_iota(I32, (n_loc, tb), 0).astype(F32)
    pick = jnp.zeros((n_loc, tb), F32)
    for k in range(TOP_K):
        pick = jnp.where(row == slots_t[k:k + 1, :], 1.0, pick)
    pick = pick.astype(BF16)

    def make(local, glob):
        return pltpu.make_async_copy(loc_scr.at[pl.ds(local, CHUNK), :], xs_ref.at[pl.ds(glob, CHUNK), :], sem)

    @pl.when(i > 0)
    def _():
        _strip_copies(loff_s, goff_s, nch_s, i - 1, make, lambda cp: cp.wait())

    loc_scr[...] = _dot(pick, tok_ref[...].astype(BF16))
    _strip_copies(loff_s, goff_s, nch_s, i, make, lambda cp: cp.start())

    @pl.when(i == pl.num_programs(0) - 1)
    def _():
        _strip_copies(loff_s, goff_s, nch_s, i, make, lambda cp: cp.wait())


def _dispatch(tok, idx, rank, loff_vec, loff, goff, nch, pad_end, padded, *, tb, tile, n_slots):
    t, d = tok.shape
    n_loc = tb * TOP_K + N_EXPERTS * CHUNK
    tok_tile = lambda i, *_: (i, 0)
    grid_spec = pltpu.PrefetchScalarGridSpec(
        num_scalar_prefetch=5,
        grid=(t // tb,),
        in_specs=[pl.BlockSpec((tb, d), tok_tile), pl.BlockSpec((tb, LANES), tok_tile),
                  pl.BlockSpec((tb, LANES), tok_tile), pl.BlockSpec((1, 1, LANES), lambda i, *_: (i, 0, 0))],
        out_specs=pl.BlockSpec(memory_space=pl.ANY),
        scratch_shapes=[pltpu.VMEM((n_loc, d), F32), pltpu.VMEM((tile, d), F32),
                        pltpu.SemaphoreType.DMA(()), pltpu.SemaphoreType.DMA(())],
    )
    return pl.pallas_call(
        functools.partial(_dispatch_kernel, tile=tile),
        grid_spec=grid_spec,
        out_shape=jax.ShapeDtypeStruct((n_slots, d), F32),
        compiler_params=_params("arbitrary"),
        name="dispatch",
    )(loff, goff, nch, pad_end, padded, tok, idx, rank, loff_vec)


def _expert_kernel(te_ref, nu_ref, xs_ref, w1_ref, b1_ref, w2_ref, b2_ref, ys_ref, w1_bf, w2_bf, *, ff):
    i = pl.program_id(0)

    @pl.when(i >= nu_ref[0])
    def _():
        ys_ref[...] = jnp.zeros_like(ys_ref)

    @pl.when(i < nu_ref[0])
    def _():
        @pl.when(jnp.logical_or(i == 0, te_ref[i] != te_ref[jnp.maximum(i - 1, 0)]))
        def _():
            w1_bf[...] = w1_ref[...].astype(BF16)
            w2_bf[...] = w2_ref[...].astype(BF16)

        hh = _dot(xs_ref[...].astype(BF16), w1_bf[...]) + b1_ref[...]
        glu = jnp.minimum(hh[:, :ff], SWIGLU_LIMIT)
        lin = jnp.clip(hh[:, ff:], -SWIGLU_LIMIT, SWIGLU_LIMIT)
        act = glu * _sigmoid(SWIGLU_ALPHA * glu) * (lin + 1.0)
        ys_ref[...] = _dot(act.astype(BF16), w2_bf[...]) + b2_ref[...]


def _experts(xs, w1, b1, w2, b2, tile_expert, n_used, *, tile, layer):
    n_slots, d = xs.shape
    depth, ne, _, ff2 = w1.shape
    ff = ff2 // 2
    row = lambda i, te, nu: (jnp.minimum(i, nu[0] - 1), 0)
    per_e = lambda i, te, nu: (layer, te[i], 0, 0)
    grid_spec = pltpu.PrefetchScalarGridSpec(
        num_scalar_prefetch=2,
        grid=(n_slots // tile,),
        in_specs=[pl.BlockSpec((tile, d), row),
                  pl.BlockSpec((None, None, d, ff2), per_e), pl.BlockSpec((None, None, 1, ff2), per_e),
                  pl.BlockSpec((None, None, ff, d), per_e), pl.BlockSpec((None, None, 1, d), per_e)],
        out_specs=pl.BlockSpec((tile, d), lambda i, te, nu: (i, 0)),
        scratch_shapes=[pltpu.VMEM((d, ff2), BF16), pltpu.VMEM((ff, d), BF16)],
    )
    return pl.pallas_call(
        functools.partial(_expert_kernel, ff=ff),
        grid_spec=grid_spec,
        out_shape=jax.ShapeDtypeStruct((n_slots, d), F32),
        compiler_params=_params("arbitrary"),
        name="experts",
    )(tile_expert, n_used, xs, w1, b1.reshape(depth, ne, 1, ff2), w2, b2.reshape(depth, ne, 1, d))


def _combine_kernel(loff_s, goff_s, nch_s, idx_ref, rank_ref, wts_ref, loff_ref, x1_ref, mod_ref, lg_ref, lb_ref,
                    ys_ref, o_ref, loc_scr, sem, *, d, alpha):
    i = pl.program_id(0)
    tb = x1_ref.shape[0]
    n_loc = loc_scr.shape[0]
    loc_scr[...] = jnp.zeros_like(loc_scr)

    def make(local, glob):
        return pltpu.make_async_copy(ys_ref.at[pl.ds(glob, CHUNK), :], loc_scr.at[pl.ds(local, CHUNK), :], sem)

    _strip_copies(loff_s, goff_s, nch_s, i, make, lambda cp: cp.start())

    slots = _local_slots(idx_ref, rank_ref, loff_ref)
    w = wts_ref[...]
    col = lax.broadcasted_iota(I32, (tb, n_loc), 1).astype(F32)
    wmat = jnp.zeros((tb, n_loc), F32)
    for k in range(TOP_K):
        wmat = jnp.where(col == slots[k], w[:, k:k + 1], wmat)
    wmat = wmat.astype(BF16)

    _strip_copies(loff_s, goff_s, nch_s, i, make, lambda cp: cp.wait())
    y = _dot(wmat, loc_scr[...].astype(BF16))
    m = mod_ref[0]
    o_ref[...] = _ln(alpha * x1_ref[...] + m[:, 5 * d:6 * d] * y) * lg_ref[...] + lb_ref[...]


def _combine(ys, idx, rank, wts, loff_vec, loff, goff, nch, x1, mod3, ln_g, ln_b, mod_row, *, tb, alpha):
    t, d = x1.shape
    n_loc = tb * TOP_K + N_EXPERTS * CHUNK
    tile = lambda i, *_: (i, 0)
    const2 = lambda i, *_: (0, 0)
    grid_spec = pltpu.PrefetchScalarGridSpec(
        num_scalar_prefetch=3,
        grid=(t // tb,),
        in_specs=[pl.BlockSpec((tb, LANES), tile), pl.BlockSpec((tb, LANES), tile), pl.BlockSpec((tb, LANES), tile),
                  pl.BlockSpec((1, 1, LANES), lambda i, *_: (i, 0, 0)),
                  pl.BlockSpec((tb, d), tile),
                  pl.BlockSpec((1, 1, 6 * d), lambda i, *_: (mod_row(i, tb), 0, 0)),
                  pl.BlockSpec((1, d), const2), pl.BlockSpec((1, d), const2),
                  pl.BlockSpec(memory_space=pl.ANY)],
        out_specs=pl.BlockSpec((tb, d), tile),
        scratch_shapes=[pltpu.VMEM((n_loc, d), F32), pltpu.SemaphoreType.DMA(())],
    )
    return pl.pallas_call(
        functools.partial(_combine_kernel, d=d, alpha=alpha),
        grid_spec=grid_spec,
        out_shape=jax.ShapeDtypeStruct((t, d), F32),
        compiler_params=_params("arbitrary"),
        name="combine",
    )(loff, goff, nch, idx, rank, wts, loff_vec, x1, mod3, ln_g.reshape(1, d), ln_b.reshape(1, d), ys)


def _moe(tok, x1, mod3, mod_row, router_w, router_b, w1, b1, w2, b2, ln_g, ln_b, *, alpha, layer):
    t = tok.shape[0]
    tile, tb = MOE_TILE, MOE_BLOCK
    nblk = t // tb
    idx, wts, rank, counts = _router(tok, router_w, router_b, tb=tb, tm=math.gcd(2 * tb, t))
    n = counts[:, 0, :N_EXPERTS].astype(I32)
    npad = (n + CHUNK - 1) // CHUNK * CHUNK
    loff = jnp.cumsum(npad, axis=1) - npad
    padded = (jnp.sum(npad, axis=0) + tile - 1) // tile * tile
    pad_end = jnp.cumsum(padded)
    goff = (pad_end - padded)[None, :] + jnp.cumsum(npad, axis=0) - npad
    nch = npad // CHUNK
    loff_vec = jnp.zeros((nblk, 1, LANES), I32).at[:, 0, :N_EXPERTS].set(loff)
    flat = lambda a: a.reshape(-1).astype(I32)
    n_tiles = -(-(t * TOP_K + (CHUNK - 1) * N_EXPERTS * nblk) // tile) + N_EXPERTS
    n_used = (pad_end[-1] // tile).astype(I32).reshape(1)
    tile_start = jnp.minimum(jnp.arange(n_tiles, dtype=I32), n_used[0] - 1) * tile
    tile_expert = jnp.minimum(jnp.sum(pad_end[None, :] <= tile_start[:, None], axis=1), N_EXPERTS - 1).astype(I32)
    xs = _dispatch(tok, idx, rank, loff_vec, flat(loff), flat(goff), flat(nch), flat(pad_end), flat(padded),
                   tb=tb, tile=tile, n_slots=n_tiles * tile)
    ys = _experts(xs, w1, b1, w2, b2, tile_expert, n_used, tile=tile, layer=layer)
    return _combine(ys, idx, rank, wts, loff_vec, flat(loff), flat(goff), flat(nch), x1, mod3, ln_g, ln_b, mod_row,
                    tb=tb, alpha=alpha)


def kernel(x, c, ctx, c_ctx, w_ada, b_ada, w_in, conv_dw_w, conv_dw_b, conv_ln_g, conv_ln_b, lam_q1, lam_k1, lam_q2,
           lam_k2, diff_subln_g, nat_rpb, w_branch, w_out, ln1_g, ln1_b, router_w, router_b, exp_w1, exp_b1, exp_w2,
           exp_b2, ln2_g, ln2_b):
    batch, seq, d = x.shape
    n_ctx = ctx.shape[1]
    depth = w_ada.shape[0]
    half = d // 2
    nlat, nctx = batch * seq, batch * n_ctx
    alpha = (2 * depth) ** 0.25
    assert seq % (GRID_W * NAT_QROWS) == 0 and seq // GRID_W >= NAT_SPAN_ROWS and seq % n_ctx == 0

    x_parts = (x.reshape(nlat, d), ctx.reshape(nctx, d))
    w_in_bf, w_branch_bf, w_out_bf = w_in.astype(BF16), w_branch.astype(BF16), w_out.astype(BF16)
    mod_rows = -(-(batch + 1) // SUBLANES) * SUBLANES
    cc = jnp.zeros((mod_rows, d), F32).at[:batch].set(c).at[batch].set(c_ctx)
    mod3 = _ada(cc, w_ada, b_ada).reshape(depth * mod_rows, 1, 6 * d)
    cos, sin = _rope_tables(seq, half // DIFF_D)

    for i in range(depth):
        with_ctx = i < depth - 1
        lam_init = 0.8 - 0.6 * math.exp(-0.3 * i)
        n_tok = nlat + nctx if with_ctx else nlat

        def mod_row(t, tm, i=i):
            return i * mod_rows + jnp.where(t * tm < nlat, (t * tm) // seq, batch)

        u = _inproj(x_parts, mod3, w_in_bf, mod_row, n_rows=nlat + nctx, tm=math.gcd(1024, nctx), tn=1792, layer=i)

        conv_p = (conv_dw_w[i], conv_dw_b[i], conv_ln_g[i], conv_ln_b[i])
        ya = _conv(u, *conv_p, seq=seq, nseq=batch, row_block0=0)
        q_rot, k_rot = _rope(u, cos, sin, nlat=nlat, seq=seq, tm=1024, qcol=2, kcol=3)
        lam_vecs = jnp.stack([lam_q1[i], lam_k1[i], lam_q2[i], lam_k2[i]]).astype(F32)
        yb = _diff_attn(q_rot, k_rot, u, lam_vecs, diff_subln_g[i], batch=batch, seq=seq, n_ctx=n_ctx, tq=512,
                        lam_init=lam_init, kcol=3, vcol=4)
        bias = _nat_bias_tables(nat_rpb[i], seq // GRID_W)
        yc = _nat(u, bias, batch=batch, seq=seq, n_ctx=n_ctx, qcol=5, kcol=6, vcol=7)
        y_lat = (ya, yb, yc)
        y_ctx = y_lat
        if with_ctx:
            ya_c = _conv(u, *conv_p, seq=n_ctx, nseq=batch, row_block0=nlat // n_ctx)
            yb_c, yc_c = _ctx_attn(u, lam_vecs, diff_subln_g[i], batch=batch, seq=seq, n_ctx=n_ctx, width=half,
                                   lam_init=lam_init, cols=(2, 3, 4, 5, 6, 7))
            y_ctx = (ya_c, yb_c, yc_c)

        x1, tok = _merge(y_lat, y_ctx, u, x_parts, mod3, w_branch_bf, w_out_bf, ln1_g[i], ln1_b[i], mod_row,
                         n_tok=n_tok, tm=math.gcd(512, nctx), gcol0=4, alpha=alpha, layer=i)
        xa = _moe(tok, x1, mod3, mod_row, router_w[i], router_b[i], exp_w1, exp_b1, exp_w2, exp_b2, ln2_g[i], ln2_b[i],
                  alpha=alpha, layer=i)
        x_parts = (xa, xa)
    return xa[:nlat].reshape(batch, seq, d)
```

```python
import functools
import math

import numpy as np
import jax
import jax.numpy as jnp
from jax import lax
from jax.experimental import pallas as pl
from jax.experimental.pallas import tpu as pltpu

F32 = jnp.float32
BF16 = jnp.bfloat16
I32 = jnp.int32

GRID_W = 64
CONV_K = 31
DIFF_D = 64
HEAD_DIM = 64
NAT_KR = 8
NAT_KC = 16
N_EXPERTS = 32
TOP_K = 4
SWIGLU_LIMIT = 7.0
SWIGLU_ALPHA = 1.702
ROPE_BASE = 10000.0
LN_EPS = 1e-6
NEG_BIG = -1e30

LANES = 128
SUBLANES = 8
VMEM_LIMIT = 56 * 1024 * 1024

NAT_QROWS = 4
NAT_SPAN_ROWS = NAT_QROWS + NAT_KR
MOE_TILE = 512
MOE_BLOCK = 256
CHUNK = SUBLANES


def _sigmoid(x):
    return 1.0 / (1.0 + jnp.exp(-x))


def _ln(x):
    mu = jnp.mean(x, axis=-1, keepdims=True)
    xc = x - mu
    var = jnp.mean(xc * xc, axis=-1, keepdims=True)
    return xc * lax.rsqrt(var + LN_EPS)


def _dot(a, b):
    return jnp.dot(a, b, preferred_element_type=F32)


def _dot_nt(a, b):
    return lax.dot_general(a, b, (((1,), (1,)), ((), ())), preferred_element_type=F32)


def _params(*sem):
    return pltpu.CompilerParams(dimension_semantics=sem, vmem_limit_bytes=VMEM_LIMIT)


def _ada_kernel(cc_ref, w_ref, b_ref, o_ref):
    cc = cc_ref[...]
    s = cc * _sigmoid(cc)
    o_ref[0] = _dot(s.astype(BF16), w_ref[0].astype(BF16)) + b_ref[0]


def _ada(cc, w_ada, b_ada):
    depth, d, n = w_ada.shape
    rows = cc.shape[0]
    tn = 1536
    return pl.pallas_call(
        _ada_kernel,
        grid=(depth, n // tn),
        in_specs=[pl.BlockSpec((rows, d), lambda l, j: (0, 0)),
                  pl.BlockSpec((1, d, tn), lambda l, j: (l, 0, j)),
                  pl.BlockSpec((1, 1, tn), lambda l, j: (l, 0, j))],
        out_specs=pl.BlockSpec((1, rows, tn), lambda l, j: (l, 0, j)),
        out_shape=jax.ShapeDtypeStruct((depth, rows, n), F32),
        compiler_params=_params("parallel", "parallel"),
        name="ada",
    )(cc, w_ada, b_ada.reshape(depth, 1, n))


def _two_source_specs(parts, tm):
    a_tiles = parts[0].shape[0] // tm
    b_tiles = parts[1].shape[0] // tm
    d = parts[0].shape[1]
    spec_a = pl.BlockSpec((tm, d), lambda i, *_: (jnp.minimum(i, a_tiles - 1), 0))
    spec_b = pl.BlockSpec((tm, d), lambda i, *_: (jnp.clip(i - a_tiles, 0, b_tiles - 1), 0))
    return [spec_a, spec_b], a_tiles


def _inproj_kernel(xa_ref, xb_ref, mod_ref, w_ref, u_ref, h_scr, *, d, a_tiles):
    @pl.when(pl.program_id(1) == 0)
    def _():
        m = mod_ref[0]
        x = jnp.where(pl.program_id(0) >= a_tiles, xb_ref[...], xa_ref[...])
        h = _ln(x) * (1.0 + m[:, d:2 * d]) + m[:, 0:d]
        h_scr[...] = h.astype(BF16)

    u_ref[...] = _dot(h_scr[...], w_ref[...]).astype(BF16)


def _inproj(x_parts, mod3, w_bf, mod_row, *, n_rows, tm, tn, layer):
    d = x_parts[0].shape[1]
    n = w_bf.shape[2]
    x_specs, a_tiles = _two_source_specs(x_parts, tm)
    return pl.pallas_call(
        functools.partial(_inproj_kernel, d=d, a_tiles=a_tiles),
        grid=(n_rows // tm, n // tn),
        in_specs=x_specs + [pl.BlockSpec((1, 1, 6 * d), lambda i, j: (mod_row(i, tm), 0, 0)),
                            pl.BlockSpec((None, d, tn), lambda i, j: (layer, 0, j))],
        out_specs=pl.BlockSpec((tm, tn), lambda i, j: (i, j)),
        out_shape=jax.ShapeDtypeStruct((n_rows, n), BF16),
        scratch_shapes=[pltpu.VMEM((tm, d), BF16)],
        compiler_params=_params("parallel", "arbitrary"),
        name="inproj",
    )(*x_parts, mod3, w_bf)


CONV_PAD = 16
CONV_ROWS = 64


def _conv_kernel(val_ref, gate_ref, w_ref, b_ref, g_ref, beta_ref, o_ref, hp_scr, *, seq, width):
    zeros = jnp.zeros((CONV_PAD, width), F32)
    hp_scr[0:CONV_PAD, :] = zeros
    hp_scr[seq + CONV_PAD:seq + 2 * CONV_PAD, :] = zeros
    hp_scr[CONV_PAD:seq + CONV_PAD, :] = val_ref[...].astype(F32) * _sigmoid(gate_ref[...].astype(F32))
    w = w_ref[...]
    shift = CONV_PAD - CONV_K // 2

    def chunk(ci, carry):
        t0 = pl.multiple_of(ci * CONV_ROWS, CONV_ROWS)
        strips = []
        for s in range(width // LANES):
            win = hp_scr[pl.ds(t0, CONV_ROWS + 2 * CONV_PAD), s * LANES:(s + 1) * LANES]
            acc = jnp.zeros((CONV_ROWS, LANES), F32)
            for r in range(SUBLANES):
                shifted = win if r == 0 else pltpu.roll(win, win.shape[0] - r, 0)
                for base in range(0, 2 * CONV_PAD, SUBLANES):
                    k = base + r - shift
                    if 0 <= k < CONV_K:
                        acc = acc + w[k:k + 1, s * LANES:(s + 1) * LANES] * shifted[base:base + CONV_ROWS, :]
            strips.append(acc)
        y = jnp.concatenate(strips, axis=1) + b_ref[...]
        y = _ln(y) * g_ref[...] + beta_ref[...]
        o_ref[pl.ds(t0, CONV_ROWS), :] = (y * _sigmoid(y)).astype(BF16)
        return carry

    lax.fori_loop(0, seq // CONV_ROWS, chunk, 0)


def _conv(u, dw_w, dw_b, ln_g, ln_b, *, seq, nseq, row_block0):
    width = dw_w.shape[1]
    wpad = jnp.zeros((32, width), F32).at[:CONV_K].set(dw_w)
    const2 = lambda b: (0, 0)
    return pl.pallas_call(
        functools.partial(_conv_kernel, seq=seq, width=width),
        grid=(nseq,),
        in_specs=[pl.BlockSpec((seq, width), lambda b: (row_block0 + b, 0)),
                  pl.BlockSpec((seq, width), lambda b: (row_block0 + b, 1)),
                  pl.BlockSpec((32, width), const2),
                  pl.BlockSpec((1, width), const2), pl.BlockSpec((1, width), const2),
                  pl.BlockSpec((1, width), const2)],
        out_specs=pl.BlockSpec((seq, width), lambda b: (b, 0)),
        out_shape=jax.ShapeDtypeStruct((nseq * seq, width), BF16),
        scratch_shapes=[pltpu.VMEM((seq + 2 * CONV_PAD, width), F32)],
        compiler_params=_params("parallel"),
        name="conv",
    )(u, u, wpad, dw_b.reshape(1, width), ln_g.reshape(1, width), ln_b.reshape(1, width))


def _rope_kernel(q_ref, k_ref, cos_ref, sin_ref, qo_ref, ko_ref, *, scale):
    cos = cos_ref[...]
    sin = sin_ref[...]
    width = cos.shape[1]
    lane = lax.broadcasted_iota(I32, cos.shape, 1)
    first = (lane & (DIFF_D // 2 - 1)) < (DIFF_D // 4)

    def rot(x):
        partner = jnp.where(first, pltpu.roll(x, width - DIFF_D // 4, 1), pltpu.roll(x, DIFF_D // 4, 1))
        return x * cos + partner * sin

    qo_ref[...] = (rot(q_ref[...].astype(F32)) * scale).astype(BF16)
    ko_ref[...] = rot(k_ref[...].astype(F32)).astype(BF16)


def _rope(u, cos, sin, *, nlat, seq, tm, qcol, kcol):
    width = cos.shape[1]
    per_seq = seq // tm
    spec_tab = pl.BlockSpec((tm, width), lambda i: (i % per_seq, 0))
    out = jax.ShapeDtypeStruct((nlat, width), BF16)
    return pl.pallas_call(
        functools.partial(_rope_kernel, scale=DIFF_D ** -0.5 * math.log2(math.e)),
        grid=(nlat // tm,),
        in_specs=[pl.BlockSpec((tm, width), lambda i: (i, qcol)),
                  pl.BlockSpec((tm, width), lambda i: (i, kcol)),
                  spec_tab, spec_tab],
        out_specs=[pl.BlockSpec((tm, width), lambda i: (i, 0))] * 2,
        out_shape=[out, out],
        compiler_params=_params("parallel"),
        name="rope",
    )(u, u, cos, sin)


def _rope_tables(seq, n_maps):
    axis_dim = DIFF_D // 2
    inv_freq = ROPE_BASE ** (-jnp.arange(0, axis_dim, 2, dtype=F32) / axis_dim)
    t = jnp.arange(seq, dtype=I32)
    ang_r = (t // GRID_W).astype(F32)[:, None] * inv_freq[None, :]
    ang_c = (t % GRID_W).astype(F32)[:, None] * inv_freq[None, :]
    cos = jnp.concatenate([jnp.cos(ang_r)] * 2 + [jnp.cos(ang_c)] * 2, axis=1)
    sin = jnp.concatenate([-jnp.sin(ang_r), jnp.sin(ang_r), -jnp.sin(ang_c), jnp.sin(ang_c)], axis=1)
    return jnp.tile(cos, (1, n_maps)), jnp.tile(sin, (1, n_maps))


def _lambda(lam_ref, lam_init):
    lv = lam_ref[...]
    return (jnp.exp(jnp.sum(lv[0:1] * lv[1:2], axis=-1, keepdims=True))
            - jnp.exp(jnp.sum(lv[2:3] * lv[3:4], axis=-1, keepdims=True)) + lam_init)


def _softmax_parts(score_parts):
    m = score_parts[0].max(axis=-1, keepdims=True)
    for s in score_parts[1:]:
        m = jnp.maximum(m, s.max(axis=-1, keepdims=True))
    es = [jnp.exp(s - m) for s in score_parts]
    tot = es[0].sum(axis=-1, keepdims=True)
    for e in es[1:]:
        tot = tot + e.sum(axis=-1, keepdims=True)
    return es, 1.0 / tot


def _diff_head(q1, q2, keys1, keys2, vals, lam, g, lam_init):
    e1, inv1 = _softmax_parts([_dot_nt(q1, k) for k in keys1])
    e2, inv2 = _softmax_parts([_dot_nt(q2, k) for k in keys2])
    o = None
    for a, b, v in zip(e1, e2, vals):
        p = (a * inv1 - lam * (b * inv2)).astype(BF16)
        o = _dot(p, v) if o is None else o + _dot(p, v)
    o = o * lax.rsqrt(jnp.mean(o * o, axis=-1, keepdims=True) + LN_EPS)
    return o * g * (1.0 - lam_init)


def _diff_kernel(q_ref, k_ref, v_ref, kc_ref, vc_ref, lam_ref, g_ref, o_ref, kall, vaug, *, lam_init, heads, n_ctx):
    hd = 2 * DIFF_D

    @pl.when(pl.program_id(1) == 0)
    def _():
        kall[0:n_ctx, :] = kc_ref[...]
        kall[n_ctx:, :] = k_ref[...]
        ones = jnp.ones((kall.shape[0], hd), BF16)
        for h in range(heads):
            vaug[h, 0:n_ctx, 0:hd] = vc_ref[:, h * hd:(h + 1) * hd]
            vaug[h, n_ctx:, 0:hd] = v_ref[:, h * hd:(h + 1) * hd]
            vaug[h, :, hd:2 * hd] = ones

    lam = _lambda(lam_ref, lam_init)
    g = g_ref[...]
    for h in range(heads):
        c0 = h * hd

        def attend(lo):
            s = _dot_nt(q_ref[:, lo:lo + DIFF_D], kall[:, lo:lo + DIFF_D])
            e = jnp.exp2(s - s.max(axis=-1, keepdims=True)).astype(BF16)
            r = _dot(e, vaug[h])
            return r[:, :hd] * (1.0 / r[:, hd:hd + 1])

        o = attend(c0) - lam * attend(c0 + DIFF_D)
        o = o * lax.rsqrt(jnp.mean(o * o, axis=-1, keepdims=True) + LN_EPS)
        o_ref[:, c0:c0 + hd] = (o * g * (1.0 - lam_init)).astype(BF16)


def _diff_attn(q_rot, k_rot, u, lam_vecs, subln_g, *, batch, seq, n_ctx, tq, lam_init, kcol, vcol):
    width = q_rot.shape[1]
    heads = width // (2 * DIFF_D)
    nq = seq // tq
    ctx_blk0 = batch * seq // n_ctx
    return pl.pallas_call(
        functools.partial(_diff_kernel, lam_init=lam_init, heads=heads, n_ctx=n_ctx),
        grid=(batch, nq),
        scratch_shapes=[pltpu.VMEM((seq + n_ctx, width), BF16),
                        pltpu.VMEM((heads, seq + n_ctx, 4 * DIFF_D), BF16)],
        in_specs=[pl.BlockSpec((tq, width), lambda b, i: (b * nq + i, 0)),
                  pl.BlockSpec((seq, width), lambda b, i: (b, 0)),
                  pl.BlockSpec((seq, width), lambda b, i: (b, vcol)),
                  pl.BlockSpec((n_ctx, width), lambda b, i: (ctx_blk0 + b, kcol)),
                  pl.BlockSpec((n_ctx, width), lambda b, i: (ctx_blk0 + b, vcol)),
                  pl.BlockSpec((4, DIFF_D), lambda b, i: (0, 0)),
                  pl.BlockSpec((1, 2 * DIFF_D), lambda b, i: (0, 0))],
        out_specs=pl.BlockSpec((tq, width), lambda b, i: (b * nq + i, 0)),
        out_shape=jax.ShapeDtypeStruct((batch * seq, width), BF16),
        compiler_params=_params("parallel", "arbitrary"),
        name="diff_attn",
    )(q_rot, k_rot, u, u, u, lam_vecs, subln_g.reshape(1, 2 * DIFF_D))


def _ctx_attn_kernel(bq_ref, bk_ref, bv_ref, cq_ref, ck_ref, cv_ref, lam_ref, g_ref, yb_ref, yc_ref, *, lam_init):
    lam = _lambda(lam_ref, lam_init)
    g = g_ref[...]
    width = bq_ref.shape[1]
    hd = 2 * DIFF_D
    scale = jnp.asarray(DIFF_D ** -0.5, BF16)
    for h in range(width // hd):
        c0 = h * hd
        o = _diff_head(bq_ref[:, c0:c0 + DIFF_D] * scale, bq_ref[:, c0 + DIFF_D:c0 + hd] * scale,
                       [bk_ref[:, c0:c0 + DIFF_D]], [bk_ref[:, c0 + DIFF_D:c0 + hd]],
                       [bv_ref[:, c0:c0 + hd]], lam, g, lam_init)
        yb_ref[:, c0:c0 + hd] = o.astype(BF16)
    scale_c = jnp.asarray(HEAD_DIM ** -0.5, BF16)
    for h in range(width // HEAD_DIM):
        c0 = h * HEAD_DIM
        es, inv = _softmax_parts([_dot_nt(cq_ref[:, c0:c0 + HEAD_DIM] * scale_c, ck_ref[:, c0:c0 + HEAD_DIM])])
        o = _dot((es[0] * inv).astype(BF16), cv_ref[:, c0:c0 + HEAD_DIM])
        yc_ref[:, c0:c0 + HEAD_DIM] = o.astype(BF16)


def _ctx_attn(u, lam_vecs, subln_g, *, batch, seq, n_ctx, width, lam_init, cols):
    blk0 = batch * seq // n_ctx
    u_specs = [pl.BlockSpec((n_ctx, width), functools.partial(lambda b, col: (blk0 + b, col), col=col))
               for col in cols]
    out_spec = pl.BlockSpec((n_ctx, width), lambda b: (b, 0))
    out = jax.ShapeDtypeStruct((batch * n_ctx, width), BF16)
    return pl.pallas_call(
        functools.partial(_ctx_attn_kernel, lam_init=lam_init),
        grid=(batch,),
        in_specs=u_specs + [pl.BlockSpec((4, DIFF_D), lambda b: (0, 0)),
                            pl.BlockSpec((1, 2 * DIFF_D), lambda b: (0, 0))],
        out_specs=[out_spec, out_spec],
        out_shape=[out, out],
        compiler_params=_params("parallel"),
        name="ctx_attn",
    )(u, u, u, u, u, u, lam_vecs, subln_g.reshape(1, 2 * DIFF_D))


def _nat_kernel(q_ref, k_ref, v_ref, kc_ref, vc_ref, bias_ref, o_ref, *, rows):
    j = pl.program_id(1)
    span = NAT_SPAN_ROWS * GRID_W
    start_row = jnp.clip(j * NAT_QROWS - NAT_KR // 2, 0, rows - NAT_SPAN_ROWS)
    start = pl.multiple_of(start_row * GRID_W, NAT_QROWS * GRID_W)
    scale = jnp.asarray(HEAD_DIM ** -0.5, BF16)
    for h in range(q_ref.shape[1] // HEAD_DIM):
        c0 = h * HEAD_DIM
        q = q_ref[:, c0:c0 + HEAD_DIM] * scale
        s_loc = _dot_nt(q, k_ref[pl.ds(start, span), c0:c0 + HEAD_DIM]) + bias_ref[0, h]
        s_ctx = _dot_nt(q, kc_ref[:, c0:c0 + HEAD_DIM])
        (e_loc, e_ctx), inv = _softmax_parts([s_loc, s_ctx])
        o = (_dot(e_loc.astype(BF16), v_ref[pl.ds(start, span), c0:c0 + HEAD_DIM])
             + _dot(e_ctx.astype(BF16), vc_ref[:, c0:c0 + HEAD_DIM]))
        o_ref[:, c0:c0 + HEAD_DIM] = (o * inv).astype(BF16)


def _nat_bias_tables(rpb, rows):
    n_groups = rows // NAT_QROWS
    tabs = []
    for j in (0, 1, n_groups - 1):
        start = int(np.clip(j * NAT_QROWS - NAT_KR // 2, 0, rows - NAT_SPAN_ROWS))
        r = j * NAT_QROWS + np.arange(NAT_QROWS)
        kr = start + np.arange(NAT_SPAN_ROWS)
        rs = np.clip(r - NAT_KR // 2, 0, rows - NAT_KR)
        row_ok = (kr[None, :] >= rs[:, None]) & (kr[None, :] < rs[:, None] + NAT_KR)
        d_row = np.clip(kr[None, :] - r[:, None] + NAT_KR - 1, 0, 2 * NAT_KR - 2)
        c = np.arange(GRID_W)
        cs = np.clip(c - NAT_KC // 2, 0, GRID_W - NAT_KC)
        col_ok = (c[None, :] >= cs[:, None]) & (c[None, :] < cs[:, None] + NAT_KC)
        d_col = np.clip(c[None, :] - c[:, None] + NAT_KC - 1, 0, 2 * NAT_KC - 2)
        pick = (d_col[:, :, None] == np.arange(2 * NAT_KC - 1)[None, None, :]).astype(np.float32)
        t = jnp.einsum("hqsd,cxd->hqcsx", rpb.astype(F32)[:, d_row], pick, precision=lax.Precision.HIGHEST)
        ok = row_ok[:, None, :, None] & col_ok[None, :, None, :]
        t = jnp.where(ok[None], t, NEG_BIG)
        tabs.append(t.reshape(rpb.shape[0], NAT_QROWS * GRID_W, NAT_SPAN_ROWS * GRID_W))
    return jnp.stack(tabs)


def _nat(u, bias, *, batch, seq, n_ctx, qcol, kcol, vcol):
    width = 8 * HEAD_DIM
    rows = seq // GRID_W
    n_groups = rows // NAT_QROWS
    tq = NAT_QROWS * GRID_W
    ctx_blk0 = batch * seq // n_ctx
    heads = bias.shape[1]

    def bias_idx(b, j):
        return (jnp.where(j == 0, 0, jnp.where(j == n_groups - 1, 2, 1)), 0, 0, 0)

    return pl.pallas_call(
        functools.partial(_nat_kernel, rows=rows),
        grid=(batch, n_groups),
        in_specs=[pl.BlockSpec((tq, width), lambda b, j: (b * n_groups + j, qcol)),
                  pl.BlockSpec((seq, width), lambda b, j: (b, kcol)),
                  pl.BlockSpec((seq, width), lambda b, j: (b, vcol)),
                  pl.BlockSpec((n_ctx, width), lambda b, j: (ctx_blk0 + b, kcol)),
                  pl.BlockSpec((n_ctx, width), lambda b, j: (ctx_blk0 + b, vcol)),
                  pl.BlockSpec((1, heads, tq, NAT_SPAN_ROWS * GRID_W), bias_idx)],
        out_specs=pl.BlockSpec((tq, width), lambda b, j: (b * n_groups + j, 0)),
        out_shape=jax.ShapeDtypeStruct((batch * seq, width), BF16),
        compiler_params=_params("parallel", "arbitrary"),
        name="nat",
    )(u, u, u, u, u, bias)


def _merge_kernel(ya_ref, yb_ref, yc_ref, ya_ctx_ref, yb_ctx_ref, yc_ctx_ref, ga_ref, gb_ref, gc_ref, xa_ref, xb_ref,
                  mod_ref, wb_ref, wo_ref, lg_ref, lb_ref, x1_ref, tok_ref, *, d, alpha, lat_tiles, a_tiles):
    m = mod_ref[0]
    is_ctx = pl.program_id(0) >= lat_tiles
    x = jnp.where(pl.program_id(0) >= a_tiles, xb_ref[...], xa_ref[...])
    ya = jnp.where(is_ctx, ya_ctx_ref[...], ya_ref[...])
    yb = jnp.where(is_ctx, yb_ctx_ref[...], yb_ref[...])
    yc = jnp.where(is_ctx, yc_ctx_ref[...], yc_ref[...])
    acc = _sigmoid(ga_ref[...].astype(F32)) * _dot(ya, wb_ref[0])
    acc = acc + _sigmoid(gb_ref[...].astype(F32)) * _dot(yb, wb_ref[1])
    acc = acc + _sigmoid(gc_ref[...].astype(F32)) * _dot(yc, wb_ref[2])
    y = _dot(acc.astype(BF16), wo_ref[...])
    x1 = _ln(alpha * x + m[:, 2 * d:3 * d] * y) * lg_ref[...] + lb_ref[...]
    x1_ref[...] = x1
    tok_ref[...] = (_ln(x1) * (1.0 + m[:, 4 * d:5 * d]) + m[:, 3 * d:4 * d]).astype(BF16)


def _merge(y_lat, y_ctx, u, x_parts, mod3, wb_bf, wo_bf, ln_g, ln_b, mod_row, *, n_tok, tm, gcol0, alpha, layer):
    d = x_parts[0].shape[1]
    x_specs, a_tiles = _two_source_specs(x_parts, tm)
    half = y_lat[0].shape[1]
    lat_tiles = y_lat[0].shape[0] // tm
    ctx_tiles = y_ctx[0].shape[0] // tm
    tile = lambda i: (i, 0)
    const2 = lambda i: (0, 0)
    lat_tile = lambda i: (jnp.minimum(i, lat_tiles - 1), 0)
    ctx_tile = lambda i: (jnp.clip(i - lat_tiles, 0, ctx_tiles - 1), 0)
    out = jax.ShapeDtypeStruct((n_tok, d), F32)
    return pl.pallas_call(
        functools.partial(_merge_kernel, d=d, alpha=alpha, lat_tiles=lat_tiles, a_tiles=a_tiles),
        grid=(n_tok // tm,),
        in_specs=[pl.BlockSpec((tm, half), lat_tile)] * 3 + [pl.BlockSpec((tm, half), ctx_tile)] * 3
                 + [pl.BlockSpec((tm, d), functools.partial(lambda i, col: (i, col), col=gcol0 + k)) for k in range(3)]
                 + x_specs
                 + [pl.BlockSpec((1, 1, 6 * d), lambda i: (mod_row(i, tm), 0, 0)),
                    pl.BlockSpec((None, 3, half, d), lambda i: (layer, 0, 0, 0)),
                    pl.BlockSpec((None, d, d), lambda i: (layer, 0, 0)),
                    pl.BlockSpec((1, d), const2), pl.BlockSpec((1, d), const2)],
        out_specs=[pl.BlockSpec((tm, d), tile)] * 2,
        out_shape=[out, jax.ShapeDtypeStruct((n_tok, d), BF16)],
        compiler_params=_params("parallel"),
        name="merge",
    )(*y_lat, *y_ctx, u, u, u, *x_parts, mod3, wb_bf, wo_bf, ln_g.reshape(1, d), ln_b.reshape(1, d))


def _router_kernel(tok_ref, w_ref, b_ref, idx_ref, wts_ref, rank_ref, cnt_ref):
    tm = tok_ref.shape[0]
    logits = _dot(tok_ref[...], w_ref[...]) + b_ref[...]
    lane = lax.broadcasted_iota(I32, logits.shape, 1).astype(F32)
    work = logits
    idxs, vals = [], []
    for _ in range(TOP_K):
        m = work.max(axis=-1, keepdims=True)
        sel = jnp.min(jnp.where(work == m, lane, float(LANES)), axis=-1, keepdims=True)
        idxs.append(sel)
        vals.append(m)
        work = jnp.where(lane == sel, -jnp.inf, work)
    es = [jnp.exp(v - vals[0]) for v in vals]
    inv = 1.0 / (es[0] + es[1] + es[2] + es[3])

    onehot = jnp.zeros(logits.shape, F32)
    for sel in idxs:
        onehot = onehot + jnp.where(lane == sel, 1.0, 0.0)
    r_i = lax.broadcasted_iota(I32, (tm, tm), 0)
    c_i = lax.broadcasted_iota(I32, (tm, tm), 1)
    n_sub = cnt_ref.shape[0]
    tb = tm // n_sub
    log_tb = tb.bit_length() - 1
    assert tb == 1 << log_tb
    same_block = (c_i >> log_tb) == (r_i >> log_tb)
    tri = jnp.where((c_i < r_i) & same_block, 1.0, 0.0).astype(BF16)
    before = _dot(tri, onehot.astype(BF16))

    idx_out = jnp.zeros(logits.shape, F32)
    wts_out = jnp.zeros(logits.shape, F32)
    rank_out = jnp.zeros(logits.shape, F32)
    for k in range(TOP_K):
        rank_k = jnp.sum(jnp.where(lane == idxs[k], before, 0.0), axis=-1, keepdims=True)
        idx_out = jnp.where(lane == float(k), idxs[k], idx_out)
        wts_out = jnp.where(lane == float(k), es[k] * inv, wts_out)
        rank_out = jnp.where(lane == float(k), rank_k, rank_out)
    idx_ref[...] = idx_out.astype(I32)
    wts_ref[...] = wts_out
    rank_ref[...] = rank_out.astype(I32)
    for sub in range(n_sub):
        cnt_ref[sub] = jnp.sum(onehot[sub * tb:(sub + 1) * tb], axis=0, keepdims=True)


def _router(tok, w_r, b_r, *, tb, tm):
    t, d = tok.shape
    n_sub = tm // tb
    w_pad = jnp.zeros((d, LANES), BF16).at[:, :N_EXPERTS].set(w_r.astype(BF16))
    b_pad = jnp.full((1, LANES), NEG_BIG, F32).at[0, :N_EXPERTS].set(b_r)
    tile = lambda i: (i, 0)
    const2 = lambda i: (0, 0)
    return pl.pallas_call(
        _router_kernel,
        grid=(t // tm,),
        in_specs=[pl.BlockSpec((tm, d), tile), pl.BlockSpec((d, LANES), const2), pl.BlockSpec((1, LANES), const2)],
        out_specs=[pl.BlockSpec((tm, LANES), tile)] * 3 + [pl.BlockSpec((n_sub, 1, LANES), lambda i: (i, 0, 0))],
        out_shape=[jax.ShapeDtypeStruct((t, LANES), I32), jax.ShapeDtypeStruct((t, LANES), F32),
                   jax.ShapeDtypeStruct((t, LANES), I32), jax.ShapeDtypeStruct((t // tb, 1, LANES), F32)],
        compiler_params=_params("parallel"),
        name="router",
    )(tok, w_pad, b_pad)


def _local_slots(idx_ref, rank_ref, loff_ref):
    idx = idx_ref[...]
    rank = rank_ref[...].astype(F32)
    loff = loff_ref[0].astype(F32)
    lane = lax.broadcasted_iota(I32, idx.shape, 1)
    slots = []
    for k in range(TOP_K):
        base = jnp.sum(jnp.where(lane == idx[:, k:k + 1], loff, 0.0), axis=-1, keepdims=True)
        slots.append(base + rank[:, k:k + 1])
    return slots


def _start_chunks(dst_ref, make):
    n = dst_ref[0, 0, dst_ref.shape[2] - 1]

    def body(f, carry):
        make(pl.multiple_of(f * CHUNK, CHUNK), pl.multiple_of(dst_ref[0, 0, f], CHUNK), CHUNK).start()
        return carry

    lax.fori_loop(0, n, body, 0)
    return n


def _wait_chunks(n, make, max_chunks):
    for b in range(max_chunks.bit_length()):
        @pl.when((lax.shift_right_logical(n, b) & 1) == 1)
        def _():
            make(0, 0, CHUNK << b).wait()


def _dispatch_kernel(pend_ref, padded_ref, dst_ref, tok_ref, idx_ref, rank_ref, loff_ref, xs_ref,
                     loc_scr, zero_scr, sem, zsem, *, tile):
    i = pl.program_id(0)
    tb = tok_ref.shape[0]
    n_loc = loc_scr.shape[0]

    def zero_copy(e):
        off = pl.multiple_of(jnp.maximum(pend_ref[e] - tile, 0), tile)
        return pltpu.make_async_copy(zero_scr, xs_ref.at[pl.ds(off, tile), :], zsem)

    def tail_copy(j):
        return pltpu.make_async_copy(zero_scr, xs_ref.at[pl.ds(pl.multiple_of(j * tile, tile), tile), :], zsem)

    @pl.when(i == 0)
    def _():
        zero_scr[...] = jnp.zeros_like(zero_scr)
        n_used = pend_ref[N_EXPERTS - 1] // tile
        n_tiles = xs_ref.shape[0] // tile
        for e in range(N_EXPERTS):
            @pl.when(padded_ref[e] > 0)
            def _():
                zero_copy(e).start()

        def start_tail(j, carry):
            tail_copy(j).start()
            return carry

        def wait_tail(j, carry):
            tail_copy(j).wait()
            return carry

        lax.fori_loop(n_used, n_tiles, start_tail, 0)
        for e in range(N_EXPERTS):
            @pl.when(padded_ref[e] > 0)
            def _():
                zero_copy(e).wait()
        lax.fori_loop(n_used, n_tiles, wait_tail, 0)

    slots = _local_slots(idx_ref, rank_ref, loff_ref)
    lane = lax.broadcasted_iota(I32, (tb, LANES), 1)
    packed = jnp.full((tb, LANES), -1.0, F32)
    for k in range(TOP_K):
        packed = jnp.where(lane == k, slots[k], packed)
    slots_t = packed.T
    row = lax.broadcasted_iota(I32, (n_loc, tb), 0).astype(F32)
    pick = jnp.zeros((n_loc, tb), F32)
    for k in range(TOP_K):
        pick = jnp.where(row == slots_t[k:k + 1, :], 1.0, pick)
    loc_scr[...] = _dot(pick.astype(BF16), tok_ref[...])

    def make(local, glob, rows):
        return pltpu.make_async_copy(loc_scr.at[pl.ds(local, rows), :], xs_ref.at[pl.ds(glob, rows), :], sem)

    n = _start_chunks(dst_ref, make)
    _wait_chunks(n, make, n_loc // CHUNK)


def _dispatch(tok, idx, rank, loff_vec, chunk_dst, pad_end, padded, *, tb, tile, n_slots):
    t, d = tok.shape
    n_loc = tb * TOP_K + N_EXPERTS * CHUNK
    tok_tile = lambda i, *_: (i, 0)
    grid_spec = pltpu.PrefetchScalarGridSpec(
        num_scalar_prefetch=2,
        grid=(t // tb,),
        in_specs=[pl.BlockSpec((1, 1, chunk_dst.shape[2]), lambda i, *_: (i, 0, 0), memory_space=pltpu.SMEM),
                  pl.BlockSpec((tb, d), tok_tile), pl.BlockSpec((tb, LANES), tok_tile),
                  pl.BlockSpec((tb, LANES), tok_tile), pl.BlockSpec((1, 1, LANES), lambda i, *_: (i, 0, 0))],
        out_specs=pl.BlockSpec(memory_space=pl.ANY),
        scratch_shapes=[pltpu.VMEM((n_loc, d), F32), pltpu.VMEM((tile, d), F32),
                        pltpu.SemaphoreType.DMA(()), pltpu.SemaphoreType.DMA(())],
    )
    return pl.pallas_call(
        functools.partial(_dispatch_kernel, tile=tile),
        grid_spec=grid_spec,
        out_shape=jax.ShapeDtypeStruct((n_slots, d), F32),
        compiler_params=_params("arbitrary"),
        name="dispatch",
    )(pad_end, padded, chunk_dst, tok, idx, rank, loff_vec)


def _expert_kernel(te_ref, nu_ref, xs_ref, w1_ref, b1_ref, w2_ref, b2_ref, ys_ref, w1_bf, w2_bf, *, ff):
    i = pl.program_id(0)

    @pl.when(i >= nu_ref[0])
    def _():
        ys_ref[...] = jnp.zeros_like(ys_ref)

    @pl.when(i < nu_ref[0])
    def _():
        @pl.when(jnp.logical_or(i == 0, te_ref[i] != te_ref[jnp.maximum(i - 1, 0)]))
        def _():
            w1_bf[...] = w1_ref[...].astype(BF16)
            w2_bf[...] = w2_ref[...].astype(BF16)

        hh = _dot(xs_ref[...].astype(BF16), w1_bf[...]) + b1_ref[...]
        glu = jnp.minimum(hh[:, :ff], SWIGLU_LIMIT)
        lin = jnp.clip(hh[:, ff:], -SWIGLU_LIMIT, SWIGLU_LIMIT)
        act = glu * _sigmoid(SWIGLU_ALPHA * glu) * (lin + 1.0)
        ys_ref[...] = _dot(act.astype(BF16), w2_bf[...]) + b2_ref[...]


def _experts(xs, w1, b1, w2, b2, tile_expert, n_used, *, tile, layer):
    n_slots, d = xs.shape
    depth, ne, _, ff2 = w1.shape
    ff = ff2 // 2
    row = lambda i, te, nu: (jnp.minimum(i, nu[0] - 1), 0)
    per_e = lambda i, te, nu: (layer, te[i], 0, 0)
    grid_spec = pltpu.PrefetchScalarGridSpec(
        num_scalar_prefetch=2,
        grid=(n_slots // tile,),
        in_specs=[pl.BlockSpec((tile, d), row),
                  pl.BlockSpec((None, None, d, ff2), per_e), pl.BlockSpec((None, None, 1, ff2), per_e),
                  pl.BlockSpec((None, None, ff, d), per_e), pl.BlockSpec((None, None, 1, d), per_e)],
        out_specs=pl.BlockSpec((tile, d), lambda i, te, nu: (i, 0)),
        scratch_shapes=[pltpu.VMEM((d, ff2), BF16), pltpu.VMEM((ff, d), BF16)],
    )
    return pl.pallas_call(
        functools.partial(_expert_kernel, ff=ff),
        grid_spec=grid_spec,
        out_shape=jax.ShapeDtypeStruct((n_slots, d), F32),
        compiler_params=_params("arbitrary"),
        name="experts",
    )(tile_expert, n_used, xs, w1, b1.reshape(depth, ne, 1, ff2), w2, b2.reshape(depth, ne, 1, d))


def _combine_kernel(dst_ref, idx_ref, rank_ref, wts_ref, loff_ref, x1_ref, mod_ref, lg_ref, lb_ref,
                    ys_ref, o_ref, loc_scr, sem, *, d, alpha):
    tb = x1_ref.shape[0]
    n_loc = loc_scr.shape[0]
    loc_scr[...] = jnp.zeros_like(loc_scr)

    def make(local, glob, rows):
        return pltpu.make_async_copy(ys_ref.at[pl.ds(glob, rows), :], loc_scr.at[pl.ds(local, rows), :], sem)

    n = _start_chunks(dst_ref, make)

    slots = _local_slots(idx_ref, rank_ref, loff_ref)
    w = wts_ref[...]
    col = lax.broadcasted_iota(I32, (tb, n_loc), 1).astype(F32)
    wmat = jnp.zeros((tb, n_loc), F32)
    for k in range(TOP_K):
        wmat = jnp.where(col == slots[k], w[:, k:k + 1], wmat)
    wmat = wmat.astype(BF16)

    _wait_chunks(n, make, n_loc // CHUNK)
    y = _dot(wmat, loc_scr[...].astype(BF16))
    m = mod_ref[0]
    o_ref[...] = _ln(alpha * x1_ref[...] + m[:, 5 * d:6 * d] * y) * lg_ref[...] + lb_ref[...]


def _combine(ys, idx, rank, wts, loff_vec, chunk_dst, x1, mod3, ln_g, ln_b, mod_row, *, tb, alpha):
    t, d = x1.shape
    n_loc = tb * TOP_K + N_EXPERTS * CHUNK
    tile = lambda i: (i, 0)
    const2 = lambda i: (0, 0)
    return pl.pallas_call(
        functools.partial(_combine_kernel, d=d, alpha=alpha),
        grid=(t // tb,),
        in_specs=[pl.BlockSpec((1, 1, chunk_dst.shape[2]), lambda i: (i, 0, 0), memory_space=pltpu.SMEM),
                  pl.BlockSpec((tb, LANES), tile), pl.BlockSpec((tb, LANES), tile), pl.BlockSpec((tb, LANES), tile),
                  pl.BlockSpec((1, 1, LANES), lambda i: (i, 0, 0)),
                  pl.BlockSpec((tb, d), tile),
                  pl.BlockSpec((1, 1, 6 * d), lambda i: (mod_row(i, tb), 0, 0)),
                  pl.BlockSpec((1, d), const2), pl.BlockSpec((1, d), const2),
                  pl.BlockSpec(memory_space=pl.ANY)],
        out_specs=pl.BlockSpec((tb, d), tile),
        out_shape=jax.ShapeDtypeStruct((t, d), F32),
        scratch_shapes=[pltpu.VMEM((n_loc, d), F32), pltpu.SemaphoreType.DMA(())],
        compiler_params=_params("arbitrary"),
        name="combine",
    )(chunk_dst, idx, rank, wts, loff_vec, x1, mod3, ln_g.reshape(1, d), ln_b.reshape(1, d), ys)


def _moe(tok, x1, mod3, mod_row, router_w, router_b, w1, b1, w2, b2, ln_g, ln_b, *, alpha, layer):
    t = tok.shape[0]
    tile, tb = MOE_TILE, MOE_BLOCK
    nblk = t // tb
    idx, wts, rank, counts = _router(tok, router_w, router_b, tb=tb, tm=math.gcd(2 * tb, t))
    n = counts[:, 0, :N_EXPERTS].astype(I32)
    npad = (n + CHUNK - 1) // CHUNK * CHUNK
    loff = jnp.cumsum(npad, axis=1) - npad
    padded = (jnp.sum(npad, axis=0) + tile - 1) // tile * tile
    pad_end = jnp.cumsum(padded)
    goff = (pad_end - padded)[None, :] + jnp.cumsum(npad, axis=0) - npad
    loff_vec = jnp.zeros((nblk, 1, LANES), I32).at[:, 0, :N_EXPERTS].set(loff)
    max_chunks = (tb * TOP_K + N_EXPERTS * CHUNK) // CHUNK
    ch_end = jnp.cumsum(npad, axis=1) // CHUNK
    f = jnp.arange(max_chunks, dtype=I32)
    owner = jnp.minimum(jnp.sum(ch_end[:, None, :] <= f[None, :, None], axis=2), N_EXPERTS - 1)
    dst = (jnp.take_along_axis(goff, owner, axis=1)
           + (f[None, :] - jnp.take_along_axis(loff // CHUNK, owner, axis=1)) * CHUNK)
    chunk_dst = jnp.zeros((nblk, 1, 2 * LANES), I32).at[:, 0, :max_chunks].set(dst).at[:, 0, -1].set(ch_end[:, -1])
    flat = lambda a: a.reshape(-1).astype(I32)
    n_tiles = -(-(t * TOP_K + (CHUNK - 1) * N_EXPERTS * nblk) // tile) + N_EXPERTS
    n_used = (pad_end[-1] // tile).astype(I32).reshape(1)
    tile_start = jnp.minimum(jnp.arange(n_tiles, dtype=I32), n_used[0] - 1) * tile
    tile_expert = jnp.minimum(jnp.sum(pad_end[None, :] <= tile_start[:, None], axis=1), N_EXPERTS - 1).astype(I32)
    xs = _dispatch(tok, idx, rank, loff_vec, chunk_dst, flat(pad_end), flat(padded), tb=tb, tile=tile,
                   n_slots=n_tiles * tile)
    ys = _experts(xs, w1, b1, w2, b2, tile_expert, n_used, tile=tile, layer=layer)
    return _combine(ys, idx, rank, wts, loff_vec, chunk_dst, x1, mod3, ln_g, ln_b, mod_row, tb=tb, alpha=alpha)


def kernel(x, c, ctx, c_ctx, w_ada, b_ada, w_in, conv_dw_w, conv_dw_b, conv_ln_g, conv_ln_b, lam_q1, lam_k1, lam_q2,
           lam_k2, diff_subln_g, nat_rpb, w_branch, w_out, ln1_g, ln1_b, router_w, router_b, exp_w1, exp_b1, exp_w2,
           exp_b2, ln2_g, ln2_b):
    batch, seq, d = x.shape
    n_ctx = ctx.shape[1]
    depth = w_ada.shape[0]
    half = d // 2
    nlat, nctx = batch * seq, batch * n_ctx
    alpha = (2 * depth) ** 0.25
    assert seq % (GRID_W * NAT_QROWS) == 0 and seq // GRID_W >= NAT_SPAN_ROWS and seq % n_ctx == 0

    x_parts = (x.reshape(nlat, d), ctx.reshape(nctx, d))
    w_in_bf, w_branch_bf, w_out_bf = w_in.astype(BF16), w_branch.astype(BF16), w_out.astype(BF16)
    mod_rows = -(-(batch + 1) // SUBLANES) * SUBLANES
    cc = jnp.zeros((mod_rows, d), F32).at[:batch].set(c).at[batch].set(c_ctx)
    mod3 = _ada(cc, w_ada, b_ada).reshape(depth * mod_rows, 1, 6 * d)
    cos, sin = _rope_tables(seq, half // DIFF_D)

    for i in range(depth):
        with_ctx = i < depth - 1
        lam_init = 0.8 - 0.6 * math.exp(-0.3 * i)
        n_tok = nlat + nctx if with_ctx else nlat

        def mod_row(t, tm, i=i):
            return i * mod_rows + jnp.where(t * tm < nlat, (t * tm) // seq, batch)

        u = _inproj(x_parts, mod3, w_in_bf, mod_row, n_rows=nlat + nctx, tm=math.gcd(1024, nctx), tn=1792, layer=i)

        conv_p = (conv_dw_w[i], conv_dw_b[i], conv_ln_g[i], conv_ln_b[i])
        ya = _conv(u, *conv_p, seq=seq, nseq=batch, row_block0=0)
        q_rot, k_rot = _rope(u, cos, sin, nlat=nlat, seq=seq, tm=1024, qcol=2, kcol=3)
        lam_vecs = jnp.stack([lam_q1[i], lam_k1[i], lam_q2[i], lam_k2[i]]).astype(F32)
        yb = _diff_attn(q_rot, k_rot, u, lam_vecs, diff_subln_g[i], batch=batch, seq=seq, n_ctx=n_ctx, tq=512,
                        lam_init=lam_init, kcol=3, vcol=4)
        bias = _nat_bias_tables(nat_rpb[i], seq // GRID_W)
        yc = _nat(u, bias, batch=batch, seq=seq, n_ctx=n_ctx, qcol=5, kcol=6, vcol=7)
        y_lat = (ya, yb, yc)
        y_ctx = y_lat
        if with_ctx:
            ya_c = _conv(u, *conv_p, seq=n_ctx, nseq=batch, row_block0=nlat // n_ctx)
            yb_c, yc_c = _ctx_attn(u, lam_vecs, diff_subln_g[i], batch=batch, seq=seq, n_ctx=n_ctx, width=half,
                                   lam_init=lam_init, cols=(2, 3, 4, 5, 6, 7))
            y_ctx = (ya_c, yb_c, yc_c)

        x1, tok = _merge(y_lat, y_ctx, u, x_parts, mod3, w_branch_bf, w_out_bf, ln1_g[i], ln1_b[i], mod_row,
                         n_tok=n_tok, tm=math.gcd(512, nctx), gcol0=4, alpha=alpha, layer=i)
        xa = _moe(tok, x1, mod3, mod_row, router_w[i], router_b[i], exp_w1, exp_b1, exp_w2, exp_b2, ln2_g[i], ln2_b[i],
                  alpha=alpha, layer=i)
        x_parts = (xa, xa)
    return xa[:nlat].reshape(batch, seq, d)
```

```python
import functools
import math

import numpy as np
import jax
import jax.numpy as jnp
from jax import lax
from jax.experimental import pallas as pl
from jax.experimental.pallas import tpu as pltpu

F32 = jnp.float32
BF16 = jnp.bfloat16
I32 = jnp.int32

GRID_W = 64
CONV_K = 31
DIFF_D = 64
HEAD_DIM = 64
NAT_KR = 8
NAT_KC = 16
N_EXPERTS = 32
TOP_K = 4
SWIGLU_LIMIT = 7.0
SWIGLU_ALPHA = 1.702
ROPE_BASE = 10000.0
LN_EPS = 1e-6
NEG_BIG = -1e30

LANES = 128
SUBLANES = 8
VMEM_LIMIT = 56 * 1024 * 1024

NAT_QROWS = 4
NAT_SPAN_ROWS = NAT_QROWS + NAT_KR
MOE_TILE = 512
MOE_BLOCK = 256
CHUNK = SUBLANES


def _sigmoid(x):
    return 1.0 / (1.0 + jnp.exp(-x))


def _ln(x):
    mu = jnp.mean(x, axis=-1, keepdims=True)
    xc = x - mu
    var = jnp.mean(xc * xc, axis=-1, keepdims=True)
    return xc * lax.rsqrt(var + LN_EPS)


def _dot(a, b):
    return jnp.dot(a, b, preferred_element_type=F32)


def _dot_nt(a, b):
    return lax.dot_general(a, b, (((1,), (1,)), ((), ())), preferred_element_type=F32)


def _params(*sem):
    return pltpu.CompilerParams(dimension_semantics=sem, vmem_limit_bytes=VMEM_LIMIT)


def _ada_kernel(cc_ref, w_ref, b_ref, o_ref):
    cc = cc_ref[...]
    s = cc * _sigmoid(cc)
    o_ref[0] = _dot(s.astype(BF16), w_ref[0].astype(BF16)) + b_ref[0]


def _ada(cc, w_ada, b_ada):
    depth, d, n = w_ada.shape
    rows = cc.shape[0]
    tn = 1536
    return pl.pallas_call(
        _ada_kernel,
        grid=(depth, n // tn),
        in_specs=[pl.BlockSpec((rows, d), lambda l, j: (0, 0)),
                  pl.BlockSpec((1, d, tn), lambda l, j: (l, 0, j)),
                  pl.BlockSpec((1, 1, tn), lambda l, j: (l, 0, j))],
        out_specs=pl.BlockSpec((1, rows, tn), lambda l, j: (l, 0, j)),
        out_shape=jax.ShapeDtypeStruct((depth, rows, n), F32),
        compiler_params=_params("parallel", "parallel"),
        name="ada",
    )(cc, w_ada, b_ada.reshape(depth, 1, n))


def _two_source_specs(parts, tm):
    a_tiles = parts[0].shape[0] // tm
    b_tiles = parts[1].shape[0] // tm
    d = parts[0].shape[1]
    spec_a = pl.BlockSpec((tm, d), lambda i, *_: (jnp.minimum(i, a_tiles - 1), 0))
    spec_b = pl.BlockSpec((tm, d), lambda i, *_: (jnp.clip(i - a_tiles, 0, b_tiles - 1), 0))
    return [spec_a, spec_b], a_tiles


def _inproj_kernel(xa_ref, xb_ref, mod_ref, w_ref, u_ref, h_scr, *, d, a_tiles):
    @pl.when(pl.program_id(1) == 0)
    def _():
        m = mod_ref[0]
        x = jnp.where(pl.program_id(0) >= a_tiles, xb_ref[...], xa_ref[...])
        h = _ln(x) * (1.0 + m[:, d:2 * d]) + m[:, 0:d]
        h_scr[...] = h.astype(BF16)

    u_ref[...] = _dot(h_scr[...], w_ref[...]).astype(BF16)


def _inproj(x_parts, mod3, w_bf, mod_row, *, n_rows, tm, tn, layer):
    d = x_parts[0].shape[1]
    n = w_bf.shape[2]
    x_specs, a_tiles = _two_source_specs(x_parts, tm)
    return pl.pallas_call(
        functools.partial(_inproj_kernel, d=d, a_tiles=a_tiles),
        grid=(n_rows // tm, n // tn),
        in_specs=x_specs + [pl.BlockSpec((1, 1, 6 * d), lambda i, j: (mod_row(i, tm), 0, 0)),
                            pl.BlockSpec((None, d, tn), lambda i, j: (layer, 0, j))],
        out_specs=pl.BlockSpec((tm, tn), lambda i, j: (i, j)),
        out_shape=jax.ShapeDtypeStruct((n_rows, n), BF16),
        scratch_shapes=[pltpu.VMEM((tm, d), BF16)],
        compiler_params=_params("parallel", "arbitrary"),
        name="inproj",
    )(*x_parts, mod3, w_bf)


CONV_PAD = 16
CONV_ROWS = 64


def _conv_kernel(val_ref, gate_ref, w_ref, b_ref, g_ref, beta_ref, o_ref, hp_scr, *, seq, width):
    zeros = jnp.zeros((CONV_PAD, width), F32)
    hp_scr[0:CONV_PAD, :] = zeros
    hp_scr[seq + CONV_PAD:seq + 2 * CONV_PAD, :] = zeros
    hp_scr[CONV_PAD:seq + CONV_PAD, :] = val_ref[...].astype(F32) * _sigmoid(gate_ref[...].astype(F32))
    w = w_ref[...]
    shift = CONV_PAD - CONV_K // 2

    def chunk(ci, carry):
        t0 = pl.multiple_of(ci * CONV_ROWS, CONV_ROWS)
        strips = []
        for s in range(width // LANES):
            win = hp_scr[pl.ds(t0, CONV_ROWS + 2 * CONV_PAD), s * LANES:(s + 1) * LANES]
            acc = jnp.zeros((CONV_ROWS, LANES), F32)
            for r in range(SUBLANES):
                shifted = win if r == 0 else pltpu.roll(win, win.shape[0] - r, 0)
                for base in range(0, 2 * CONV_PAD, SUBLANES):
                    k = base + r - shift
                    if 0 <= k < CONV_K:
                        acc = acc + w[k:k + 1, s * LANES:(s + 1) * LANES] * shifted[base:base + CONV_ROWS, :]
            strips.append(acc)
        y = jnp.concatenate(strips, axis=1) + b_ref[...]
        y = _ln(y) * g_ref[...] + beta_ref[...]
        o_ref[pl.ds(t0, CONV_ROWS), :] = (y * _sigmoid(y)).astype(BF16)
        return carry

    lax.fori_loop(0, seq // CONV_ROWS, chunk, 0)


def _conv(u, dw_w, dw_b, ln_g, ln_b, *, seq, nseq, row_block0):
    width = dw_w.shape[1]
    wpad = jnp.zeros((32, width), F32).at[:CONV_K].set(dw_w)
    const2 = lambda b: (0, 0)
    return pl.pallas_call(
        functools.partial(_conv_kernel, seq=seq, width=width),
        grid=(nseq,),
        in_specs=[pl.BlockSpec((seq, width), lambda b: (row_block0 + b, 0)),
                  pl.BlockSpec((seq, width), lambda b: (row_block0 + b, 1)),
                  pl.BlockSpec((32, width), const2),
                  pl.BlockSpec((1, width), const2), pl.BlockSpec((1, width), const2),
                  pl.BlockSpec((1, width), const2)],
        out_specs=pl.BlockSpec((seq, width), lambda b: (b, 0)),
        out_shape=jax.ShapeDtypeStruct((nseq * seq, width), BF16),
        scratch_shapes=[pltpu.VMEM((seq + 2 * CONV_PAD, width), F32)],
        compiler_params=_params("parallel"),
        name="conv",
    )(u, u, wpad, dw_b.reshape(1, width), ln_g.reshape(1, width), ln_b.reshape(1, width))


def _rope_kernel(q_ref, k_ref, cos_ref, sin_ref, qo_ref, ko_ref, *, scale):
    cos = cos_ref[...]
    sin = sin_ref[...]
    width = cos.shape[1]
    lane = lax.broadcasted_iota(I32, cos.shape, 1)
    first = (lane & (DIFF_D // 2 - 1)) < (DIFF_D // 4)

    def rot(x):
        partner = jnp.where(first, pltpu.roll(x, width - DIFF_D // 4, 1), pltpu.roll(x, DIFF_D // 4, 1))
        return x * cos + partner * sin

    qo_ref[...] = (rot(q_ref[...].astype(F32)) * scale).astype(BF16)
    ko_ref[...] = rot(k_ref[...].astype(F32)).astype(BF16)


def _rope(u, cos, sin, *, nlat, seq, tm, qcol, kcol):
    width = cos.shape[1]
    per_seq = seq // tm
    spec_tab = pl.BlockSpec((tm, width), lambda i: (i % per_seq, 0))
    out = jax.ShapeDtypeStruct((nlat, width), BF16)
    return pl.pallas_call(
        functools.partial(_rope_kernel, scale=DIFF_D ** -0.5 * math.log2(math.e)),
        grid=(nlat // tm,),
        in_specs=[pl.BlockSpec((tm, width), lambda i: (i, qcol)),
                  pl.BlockSpec((tm, width), lambda i: (i, kcol)),
                  spec_tab, spec_tab],
        out_specs=[pl.BlockSpec((tm, width), lambda i: (i, 0))] * 2,
        out_shape=[out, out],
        compiler_params=_params("parallel"),
        name="rope",
    )(u, u, cos, sin)


def _rope_tables(seq, n_maps):
    axis_dim = DIFF_D // 2
    inv_freq = ROPE_BASE ** (-jnp.arange(0, axis_dim, 2, dtype=F32) / axis_dim)
    t = jnp.arange(seq, dtype=I32)
    ang_r = (t // GRID_W).astype(F32)[:, None] * inv_freq[None, :]
    ang_c = (t % GRID_W).astype(F32)[:, None] * inv_freq[None, :]
    cos = jnp.concatenate([jnp.cos(ang_r)] * 2 + [jnp.cos(ang_c)] * 2, axis=1)
    sin = jnp.concatenate([-jnp.sin(ang_r), jnp.sin(ang_r), -jnp.sin(ang_c), jnp.sin(ang_c)], axis=1)
    return jnp.tile(cos, (1, n_maps)), jnp.tile(sin, (1, n_maps))


def _lambda(lam_ref, lam_init):
    lv = lam_ref[...]
    return (jnp.exp(jnp.sum(lv[0:1] * lv[1:2], axis=-1, keepdims=True))
            - jnp.exp(jnp.sum(lv[2:3] * lv[3:4], axis=-1, keepdims=True)) + lam_init)


def _softmax_parts(score_parts):
    m = score_parts[0].max(axis=-1, keepdims=True)
    for s in score_parts[1:]:
        m = jnp.maximum(m, s.max(axis=-1, keepdims=True))
    es = [jnp.exp(s - m) for s in score_parts]
    tot = es[0].sum(axis=-1, keepdims=True)
    for e in es[1:]:
        tot = tot + e.sum(axis=-1, keepdims=True)
    return es, 1.0 / tot


def _diff_head(q1, q2, keys1, keys2, vals, lam, g, lam_init):
    e1, inv1 = _softmax_parts([_dot_nt(q1, k) for k in keys1])
    e2, inv2 = _softmax_parts([_dot_nt(q2, k) for k in keys2])
    o = None
    for a, b, v in zip(e1, e2, vals):
        p = (a * inv1 - lam * (b * inv2)).astype(BF16)
        o = _dot(p, v) if o is None else o + _dot(p, v)
    o = o * lax.rsqrt(jnp.mean(o * o, axis=-1, keepdims=True) + LN_EPS)
    return o * g * (1.0 - lam_init)


def _diff_kernel(q_ref, k_ref, v_ref, kc_ref, vc_ref, lam_ref, g_ref, o_ref, kall, vaug, *, lam_init, heads, n_ctx):
    hd = 2 * DIFF_D

    @pl.when(pl.program_id(1) == 0)
    def _():
        kall[0:n_ctx, :] = kc_ref[...]
        kall[n_ctx:, :] = k_ref[...]
        ones = jnp.ones((kall.shape[0], hd), BF16)
        for h in range(heads):
            vaug[h, 0:n_ctx, 0:hd] = vc_ref[:, h * hd:(h + 1) * hd]
            vaug[h, n_ctx:, 0:hd] = v_ref[:, h * hd:(h + 1) * hd]
            vaug[h, :, hd:2 * hd] = ones

    lam = _lambda(lam_ref, lam_init)
    g = g_ref[...]
    for h in range(heads):
        c0 = h * hd

        def attend(lo):
            s = _dot_nt(q_ref[:, lo:lo + DIFF_D], kall[:, lo:lo + DIFF_D])
            e = jnp.exp2(s - s.max(axis=-1, keepdims=True)).astype(BF16)
            r = _dot(e, vaug[h])
            return r[:, :hd] * (1.0 / r[:, hd:hd + 1])

        o = attend(c0) - lam * attend(c0 + DIFF_D)
        o = o * lax.rsqrt(jnp.mean(o * o, axis=-1, keepdims=True) + LN_EPS)
        o_ref[:, c0:c0 + hd] = (o * g * (1.0 - lam_init)).astype(BF16)


def _diff_attn(q_rot, k_rot, u, lam_vecs, subln_g, *, batch, seq, n_ctx, tq, lam_init, kcol, vcol):
    width = q_rot.shape[1]
    heads = width // (2 * DIFF_D)
    nq = seq // tq
    ctx_blk0 = batch * seq // n_ctx
    return pl.pallas_call(
        functools.partial(_diff_kernel, lam_init=lam_init, heads=heads, n_ctx=n_ctx),
        grid=(batch, nq),
        scratch_shapes=[pltpu.VMEM((seq + n_ctx, width), BF16),
                        pltpu.VMEM((heads, seq + n_ctx, 4 * DIFF_D), BF16)],
        in_specs=[pl.BlockSpec((tq, width), lambda b, i: (b * nq + i, 0)),
                  pl.BlockSpec((seq, width), lambda b, i: (b, 0)),
                  pl.BlockSpec((seq, width), lambda b, i: (b, vcol)),
                  pl.BlockSpec((n_ctx, width), lambda b, i: (ctx_blk0 + b, kcol)),
                  pl.BlockSpec((n_ctx, width), lambda b, i: (ctx_blk0 + b, vcol)),
                  pl.BlockSpec((4, DIFF_D), lambda b, i: (0, 0)),
                  pl.BlockSpec((1, 2 * DIFF_D), lambda b, i: (0, 0))],
        out_specs=pl.BlockSpec((tq, width), lambda b, i: (b * nq + i, 0)),
        out_shape=jax.ShapeDtypeStruct((batch * seq, width), BF16),
        compiler_params=_params("parallel", "arbitrary"),
        name="diff_attn",
    )(q_rot, k_rot, u, u, u, lam_vecs, subln_g.reshape(1, 2 * DIFF_D))


def _ctx_attn_kernel(bq_ref, bk_ref, bv_ref, cq_ref, ck_ref, cv_ref, lam_ref, g_ref, yb_ref, yc_ref, *, lam_init):
    lam = _lambda(lam_ref, lam_init)
    g = g_ref[...]
    width = bq_ref.shape[1]
    hd = 2 * DIFF_D
    scale = jnp.asarray(DIFF_D ** -0.5, BF16)
    for h in range(width // hd):
        c0 = h * hd
        o = _diff_head(bq_ref[:, c0:c0 + DIFF_D] * scale, bq_ref[:, c0 + DIFF_D:c0 + hd] * scale,
                       [bk_ref[:, c0:c0 + DIFF_D]], [bk_ref[:, c0 + DIFF_D:c0 + hd]],
                       [bv_ref[:, c0:c0 + hd]], lam, g, lam_init)
        yb_ref[:, c0:c0 + hd] = o.astype(BF16)
    scale_c = jnp.asarray(HEAD_DIM ** -0.5, BF16)
    for h in range(width // HEAD_DIM):
        c0 = h * HEAD_DIM
        es, inv = _softmax_parts([_dot_nt(cq_ref[:, c0:c0 + HEAD_DIM] * scale_c, ck_ref[:, c0:c0 + HEAD_DIM])])
        o = _dot((es[0] * inv).astype(BF16), cv_ref[:, c0:c0 + HEAD_DIM])
        yc_ref[:, c0:c0 + HEAD_DIM] = o.astype(BF16)


def _ctx_attn(u, lam_vecs, subln_g, *, batch, seq, n_ctx, width, lam_init, cols):
    blk0 = batch * seq // n_ctx
    u_specs = [pl.BlockSpec((n_ctx, width), functools.partial(lambda b, col: (blk0 + b, col), col=col))
               for col in cols]
    out_spec = pl.BlockSpec((n_ctx, width), lambda b: (b, 0))
    out = jax.ShapeDtypeStruct((batch * n_ctx, width), BF16)
    return pl.pallas_call(
        functools.partial(_ctx_attn_kernel, lam_init=lam_init),
        grid=(batch,),
        in_specs=u_specs + [pl.BlockSpec((4, DIFF_D), lambda b: (0, 0)),
                            pl.BlockSpec((1, 2 * DIFF_D), lambda b: (0, 0))],
        out_specs=[out_spec, out_spec],
        out_shape=[out, out],
        compiler_params=_params("parallel"),
        name="ctx_attn",
    )(u, u, u, u, u, u, lam_vecs, subln_g.reshape(1, 2 * DIFF_D))


def _nat_kernel(q_ref, k_ref, v_ref, kc_ref, vc_ref, bias_ref, o_ref, *, rows):
    j = pl.program_id(1)
    span = NAT_SPAN_ROWS * GRID_W
    start_row = jnp.clip(j * NAT_QROWS - NAT_KR // 2, 0, rows - NAT_SPAN_ROWS)
    start = pl.multiple_of(start_row * GRID_W, NAT_QROWS * GRID_W)
    scale = jnp.asarray(HEAD_DIM ** -0.5, BF16)
    for h in range(q_ref.shape[1] // HEAD_DIM):
        c0 = h * HEAD_DIM
        q = q_ref[:, c0:c0 + HEAD_DIM] * scale
        s_loc = _dot_nt(q, k_ref[pl.ds(start, span), c0:c0 + HEAD_DIM]) + bias_ref[0, h]
        s_ctx = _dot_nt(q, kc_ref[:, c0:c0 + HEAD_DIM])
        (e_loc, e_ctx), inv = _softmax_parts([s_loc, s_ctx])
        o = (_dot(e_loc.astype(BF16), v_ref[pl.ds(start, span), c0:c0 + HEAD_DIM])
             + _dot(e_ctx.astype(BF16), vc_ref[:, c0:c0 + HEAD_DIM]))
        o_ref[:, c0:c0 + HEAD_DIM] = (o * inv).astype(BF16)


def _nat_bias_tables(rpb, rows):
    n_groups = rows // NAT_QROWS
    tabs = []
    for j in (0, 1, n_groups - 1):
        start = int(np.clip(j * NAT_QROWS - NAT_KR // 2, 0, rows - NAT_SPAN_ROWS))
        r = j * NAT_QROWS + np.arange(NAT_QROWS)
        kr = start + np.arange(NAT_SPAN_ROWS)
        rs = np.clip(r - NAT_KR // 2, 0, rows - NAT_KR)
        row_ok = (kr[None, :] >= rs[:, None]) & (kr[None, :] < rs[:, None] + NAT_KR)
        d_row = np.clip(kr[None, :] - r[:, None] + NAT_KR - 1, 0, 2 * NAT_KR - 2)
        c = np.arange(GRID_W)
        cs = np.clip(c - NAT_KC // 2, 0, GRID_W - NAT_KC)
        col_ok = (c[None, :] >= cs[:, None]) & (c[None, :] < cs[:, None] + NAT_KC)
        d_col = np.clip(c[None, :] - c[:, None] + NAT_KC - 1, 0, 2 * NAT_KC - 2)
        pick = (d_col[:, :, None] == np.arange(2 * NAT_KC - 1)[None, None, :]).astype(np.float32)
        t = jnp.einsum("hqsd,cxd->hqcsx", rpb.astype(F32)[:, d_row], pick, precision=lax.Precision.HIGHEST)
        ok = row_ok[:, None, :, None] & col_ok[None, :, None, :]
        t = jnp.where(ok[None], t, NEG_BIG)
        tabs.append(t.reshape(rpb.shape[0], NAT_QROWS * GRID_W, NAT_SPAN_ROWS * GRID_W))
    return jnp.stack(tabs)


def _nat(u, bias, *, batch, seq, n_ctx, qcol, kcol, vcol):
    width = 8 * HEAD_DIM
    rows = seq // GRID_W
    n_groups = rows // NAT_QROWS
    tq = NAT_QROWS * GRID_W
    ctx_blk0 = batch * seq // n_ctx
    heads = bias.shape[1]

    def bias_idx(b, j):
        return (jnp.where(j == 0, 0, jnp.where(j == n_groups - 1, 2, 1)), 0, 0, 0)

    return pl.pallas_call(
        functools.partial(_nat_kernel, rows=rows),
        grid=(batch, n_groups),
        in_specs=[pl.BlockSpec((tq, width), lambda b, j: (b * n_groups + j, qcol)),
                  pl.BlockSpec((seq, width), lambda b, j: (b, kcol)),
                  pl.BlockSpec((seq, width), lambda b, j: (b, vcol)),
                  pl.BlockSpec((n_ctx, width), lambda b, j: (ctx_blk0 + b, kcol)),
                  pl.BlockSpec((n_ctx, width), lambda b, j: (ctx_blk0 + b, vcol)),
                  pl.BlockSpec((1, heads, tq, NAT_SPAN_ROWS * GRID_W), bias_idx)],
        out_specs=pl.BlockSpec((tq, width), lambda b, j: (b * n_groups + j, 0)),
        out_shape=jax.ShapeDtypeStruct((batch * seq, width), BF16),
        compiler_params=_params("parallel", "arbitrary"),
        name="nat",
    )(u, u, u, u, u, bias)


def _merge_kernel(ya_ref, yb_ref, yc_ref, ya_ctx_ref, yb_ctx_ref, yc_ctx_ref, ga_ref, gb_ref, gc_ref, xa_ref, xb_ref,
                  mod_ref, wb_ref, wo_ref, lg_ref, lb_ref, x1_ref, tok_ref, *, d, alpha, lat_tiles, a_tiles):
    m = mod_ref[0]
    is_ctx = pl.program_id(0) >= lat_tiles
    x = jnp.where(pl.program_id(0) >= a_tiles, xb_ref[...], xa_ref[...])
    ya = jnp.where(is_ctx, ya_ctx_ref[...], ya_ref[...])
    yb = jnp.where(is_ctx, yb_ctx_ref[...], yb_ref[...])
    yc = jnp.where(is_ctx, yc_ctx_ref[...], yc_ref[...])
    acc = _sigmoid(ga_ref[...].astype(F32)) * _dot(ya, wb_ref[0])
    acc = acc + _sigmoid(gb_ref[...].astype(F32)) * _dot(yb, wb_ref[1])
    acc = acc + _sigmoid(gc_ref[...].astype(F32)) * _dot(yc, wb_ref[2])
    y = _dot(acc.astype(BF16), wo_ref[...])
    x1 = _ln(alpha * x + m[:, 2 * d:3 * d] * y) * lg_ref[...] + lb_ref[...]
    x1_ref[...] = x1
    tok_ref[...] = (_ln(x1) * (1.0 + m[:, 4 * d:5 * d]) + m[:, 3 * d:4 * d]).astype(BF16)


def _merge(y_lat, y_ctx, u, x_parts, mod3, wb_bf, wo_bf, ln_g, ln_b, mod_row, *, n_tok, tm, gcol0, alpha, layer):
    d = x_parts[0].shape[1]
    x_specs, a_tiles = _two_source_specs(x_parts, tm)
    half = y_lat[0].shape[1]
    lat_tiles = y_lat[0].shape[0] // tm
    ctx_tiles = y_ctx[0].shape[0] // tm
    tile = lambda i: (i, 0)
    const2 = lambda i: (0, 0)
    lat_tile = lambda i: (jnp.minimum(i, lat_tiles - 1), 0)
    ctx_tile = lambda i: (jnp.clip(i - lat_tiles, 0, ctx_tiles - 1), 0)
    out = jax.ShapeDtypeStruct((n_tok, d), F32)
    return pl.pallas_call(
        functools.partial(_merge_kernel, d=d, alpha=alpha, lat_tiles=lat_tiles, a_tiles=a_tiles),
        grid=(n_tok // tm,),
        in_specs=[pl.BlockSpec((tm, half), lat_tile)] * 3 + [pl.BlockSpec((tm, half), ctx_tile)] * 3
                 + [pl.BlockSpec((tm, d), functools.partial(lambda i, col: (i, col), col=gcol0 + k)) for k in range(3)]
                 + x_specs
                 + [pl.BlockSpec((1, 1, 6 * d), lambda i: (mod_row(i, tm), 0, 0)),
                    pl.BlockSpec((None, 3, half, d), lambda i: (layer, 0, 0, 0)),
                    pl.BlockSpec((None, d, d), lambda i: (layer, 0, 0)),
                    pl.BlockSpec((1, d), const2), pl.BlockSpec((1, d), const2)],
        out_specs=[pl.BlockSpec((tm, d), tile)] * 2,
        out_shape=[out, jax.ShapeDtypeStruct((n_tok, d), BF16)],
        compiler_params=_params("parallel"),
        name="merge",
    )(*y_lat, *y_ctx, u, u, u, *x_parts, mod3, wb_bf, wo_bf, ln_g.reshape(1, d), ln_b.reshape(1, d))


def _router_kernel(tok_ref, w_ref, b_ref, idx_ref, wts_ref, rank_ref, cnt_ref):
    tm = tok_ref.shape[0]
    logits = _dot(tok_ref[...], w_ref[...]) + b_ref[...]
    lane = lax.broadcasted_iota(I32, logits.shape, 1).astype(F32)
    work = logits
    idxs, vals = [], []
    for _ in range(TOP_K):
        m = work.max(axis=-1, keepdims=True)
        sel = jnp.min(jnp.where(work == m, lane, float(LANES)), axis=-1, keepdims=True)
        idxs.append(sel)
        vals.append(m)
        work = jnp.where(lane == sel, -jnp.inf, work)
    es = [jnp.exp(v - vals[0]) for v in vals]
    inv = 1.0 / (es[0] + es[1] + es[2] + es[3])

    onehot = jnp.zeros(logits.shape, F32)
    for sel in idxs:
        onehot = onehot + jnp.where(lane == sel, 1.0, 0.0)
    r_i = lax.broadcasted_iota(I32, (tm, tm), 0)
    c_i = lax.broadcasted_iota(I32, (tm, tm), 1)
    n_sub = cnt_ref.shape[0]
    tb = tm // n_sub
    log_tb = tb.bit_length() - 1
    assert tb == 1 << log_tb
    same_block = (c_i >> log_tb) == (r_i >> log_tb)
    tri = jnp.where((c_i < r_i) & same_block, 1.0, 0.0).astype(BF16)
    before = _dot(tri, onehot.astype(BF16))

    idx_out = jnp.zeros(logits.shape, F32)
    wts_out = jnp.zeros(logits.shape, F32)
    rank_out = jnp.zeros(logits.shape, F32)
    for k in range(TOP_K):
        rank_k = jnp.sum(jnp.where(lane == idxs[k], before, 0.0), axis=-1, keepdims=True)
        idx_out = jnp.where(lane == float(k), idxs[k], idx_out)
        wts_out = jnp.where(lane == float(k), es[k] * inv, wts_out)
        rank_out = jnp.where(lane == float(k), rank_k, rank_out)
    idx_ref[...] = idx_out.astype(I32)
    wts_ref[...] = wts_out
    rank_ref[...] = rank_out.astype(I32)
    for sub in range(n_sub):
        cnt_ref[sub] = jnp.sum(onehot[sub * tb:(sub + 1) * tb], axis=0, keepdims=True)


def _router(tok, w_r, b_r, *, tb, tm):
    t, d = tok.shape
    n_sub = tm // tb
    w_pad = jnp.zeros((d, LANES), BF16).at[:, :N_EXPERTS].set(w_r.astype(BF16))
    b_pad = jnp.full((1, LANES), NEG_BIG, F32).at[0, :N_EXPERTS].set(b_r)
    tile = lambda i: (i, 0)
    const2 = lambda i: (0, 0)
    return pl.pallas_call(
        _router_kernel,
        grid=(t // tm,),
        in_specs=[pl.BlockSpec((tm, d), tile), pl.BlockSpec((d, LANES), const2), pl.BlockSpec((1, LANES), const2)],
        out_specs=[pl.BlockSpec((tm, LANES), tile)] * 3 + [pl.BlockSpec((n_sub, 1, LANES), lambda i: (i, 0, 0))],
        out_shape=[jax.ShapeDtypeStruct((t, LANES), I32), jax.ShapeDtypeStruct((t, LANES), F32),
                   jax.ShapeDtypeStruct((t, LANES), I32), jax.ShapeDtypeStruct((t // tb, 1, LANES), F32)],
        compiler_params=_params("parallel"),
        name="router",
    )(tok, w_pad, b_pad)


def _local_slots(idx_ref, rank_ref, loff_ref):
    idx = idx_ref[...]
    rank = rank_ref[...].astype(F32)
    loff = loff_ref[0].astype(F32)
    lane = lax.broadcasted_iota(I32, idx.shape, 1)
    slots = []
    for k in range(TOP_K):
        base = jnp.sum(jnp.where(lane == idx[:, k:k + 1], loff, 0.0), axis=-1, keepdims=True)
        slots.append(base + rank[:, k:k + 1])
    return slots


def _start_chunks(dst_ref, make):
    n = dst_ref[0, 0, dst_ref.shape[2] - 1]

    def body(f, carry):
        make(pl.multiple_of(f * CHUNK, CHUNK), pl.multiple_of(dst_ref[0, 0, f], CHUNK), CHUNK).start()
        return carry

    lax.fori_loop(0, n, body, 0)
    return n


def _wait_chunks(n, make, max_chunks):
    for b in range(max_chunks.bit_length()):
        @pl.when((lax.shift_right_logical(n, b) & 1) == 1)
        def _():
            make(0, 0, CHUNK << b).wait()


def _dispatch_kernel(pend_ref, padded_ref, dst_ref, tok_ref, idx_ref, rank_ref, loff_ref, xs_ref,
                     loc_scr, zero_scr, sem, zsem, *, tile):
    i = pl.program_id(0)
    tb = tok_ref.shape[0]
    n_loc = loc_scr.shape[0]

    def zero_copy(e):
        off = pl.multiple_of(jnp.maximum(pend_ref[e] - tile, 0), tile)
        return pltpu.make_async_copy(zero_scr, xs_ref.at[pl.ds(off, tile), :], zsem)

    def tail_copy(j):
        return pltpu.make_async_copy(zero_scr, xs_ref.at[pl.ds(pl.multiple_of(j * tile, tile), tile), :], zsem)

    @pl.when(i == 0)
    def _():
        zero_scr[...] = jnp.zeros_like(zero_scr)
        n_used = pend_ref[N_EXPERTS - 1] // tile
        n_tiles = xs_ref.shape[0] // tile
        for e in range(N_EXPERTS):
            @pl.when(padded_ref[e] > 0)
            def _():
                zero_copy(e).start()

        def start_tail(j, carry):
            tail_copy(j).start()
            return carry

        def wait_tail(j, carry):
            tail_copy(j).wait()
            return carry

        lax.fori_loop(n_used, n_tiles, start_tail, 0)
        for e in range(N_EXPERTS):
            @pl.when(padded_ref[e] > 0)
            def _():
                zero_copy(e).wait()
        lax.fori_loop(n_used, n_tiles, wait_tail, 0)

    slots = _local_slots(idx_ref, rank_ref, loff_ref)
    lane = lax.broadcasted_iota(I32, (tb, LANES), 1)
    packed = jnp.full((tb, LANES), -1.0, F32)
    for k in range(TOP_K):
        packed = jnp.where(lane == k, slots[k], packed)
    slots_t = packed.T
    row = lax.broadcasted_iota(I32, (n_loc, tb), 0).astype(F32)
    pick = jnp.zeros((n_loc, tb), F32)
    for k in range(TOP_K):
        pick = jnp.where(row == slots_t[k:k + 1, :], 1.0, pick)
    loc_scr[...] = _dot(pick.astype(BF16), tok_ref[...])

    def make(local, glob, rows):
        return pltpu.make_async_copy(loc_scr.at[pl.ds(local, rows), :], xs_ref.at[pl.ds(glob, rows), :], sem)

    n = _start_chunks(dst_ref, make)
    _wait_chunks(n, make, n_loc // CHUNK)


def _dispatch(tok, idx, rank, loff_vec, chunk_dst, pad_end, padded, *, tb, tile, n_slots):
    t, d = tok.shape
    n_loc = tb * TOP_K + N_EXPERTS * CHUNK
    tok_tile = lambda i, *_: (i, 0)
    grid_spec = pltpu.PrefetchScalarGridSpec(
        num_scalar_prefetch=2,
        grid=(t // tb,),
        in_specs=[pl.BlockSpec((1, 1, chunk_dst.shape[2]), lambda i, *_: (i, 0, 0), memory_space=pltpu.SMEM),
                  pl.BlockSpec((tb, d), tok_tile), pl.BlockSpec((tb, LANES), tok_tile),
                  pl.BlockSpec((tb, LANES), tok_tile), pl.BlockSpec((1, 1, LANES), lambda i, *_: (i, 0, 0))],
        out_specs=pl.BlockSpec(memory_space=pl.ANY),
        scratch_shapes=[pltpu.VMEM((n_loc, d), F32), pltpu.VMEM((tile, d), F32),
                        pltpu.SemaphoreType.DMA(()), pltpu.SemaphoreType.DMA(())],
    )
    return pl.pallas_call(
        functools.partial(_dispatch_kernel, tile=tile),
        grid_spec=grid_spec,
        out_shape=jax.ShapeDtypeStruct((n_slots, d), F32),
        compiler_params=_params("arbitrary"),
        name="dispatch",
    )(pad_end, padded, chunk_dst, tok, idx, rank, loff_vec)


def _expert_kernel(te_ref, nu_ref, xs_ref, w1_ref, b1_ref, w2_ref, b2_ref, ys_ref, w1_bf, w2_bf, *, ff):
    i = pl.program_id(0)

    @pl.when(i >= nu_ref[0])
    def _():
        ys_ref[...] = jnp.zeros_like(ys_ref)

    @pl.when(i < nu_ref[0])
    def _():
        @pl.when(jnp.logical_or(i == 0, te_ref[i] != te_ref[jnp.maximum(i - 1, 0)]))
        def _():
            w1_bf[...] = w1_ref[...].astype(BF16)
            w2_bf[...] = w2_ref[...].astype(BF16)

        hh = _dot(xs_ref[...].astype(BF16), w1_bf[...]) + b1_ref[...]
        glu = jnp.minimum(hh[:, :ff], SWIGLU_LIMIT)
        lin = jnp.clip(hh[:, ff:], -SWIGLU_LIMIT, SWIGLU_LIMIT)
        act = glu * _sigmoid(SWIGLU_ALPHA * glu) * (lin + 1.0)
        ys_ref[...] = _dot(act.astype(BF16), w2_bf[...]) + b2_ref[...]


def _experts(xs, w1, b1, w2, b2, tile_expert, n_used, *, tile, layer):
    n_slots, d = xs.shape
    depth, ne, _, ff2 = w1.shape
    ff = ff2 // 2
    row = lambda i, te, nu: (jnp.minimum(i, nu[0] - 1), 0)
    per_e = lambda i, te, nu: (layer, te[i], 0, 0)
    grid_spec = pltpu.PrefetchScalarGridSpec(
        num_scalar_prefetch=2,
        grid=(n_slots // tile,),
        in_specs=[pl.BlockSpec((tile, d), row),
                  pl.BlockSpec((None, None, d, ff2), per_e), pl.BlockSpec((None, None, 1, ff2), per_e),
                  pl.BlockSpec((None, None, ff, d), per_e), pl.BlockSpec((None, None, 1, d), per_e)],
        out_specs=pl.BlockSpec((tile, d), lambda i, te, nu: (i, 0)),
        scratch_shapes=[pltpu.VMEM((d, ff2), BF16), pltpu.VMEM((ff, d), BF16)],
    )
    return pl.pallas_call(
        functools.partial(_expert_kernel, ff=ff),
        grid_spec=grid_spec,
        out_shape=jax.ShapeDtypeStruct((n_slots, d), F32),
        compiler_params=_params("arbitrary"),
        name="experts",
    )(tile_expert, n_used, xs, w1, b1.reshape(depth, ne, 1, ff2), w2, b2.reshape(depth, ne, 1, d))


def _combine_kernel(dst_ref, nxt_ref, idx_ref, rank_ref, wts_ref, loff_ref, x1_ref, mod_ref, lg_ref, lb_ref,
                    ys_ref, o_ref, loc_scr, sem, *, d, alpha):
    i = pl.program_id(0)
    tb = x1_ref.shape[0]
    n_loc = loc_scr.shape[1]
    cur = lax.rem(i, 2)

    def maker(buf):
        def make(local, glob, rows):
            return pltpu.make_async_copy(ys_ref.at[pl.ds(glob, rows), :], loc_scr.at[buf, pl.ds(local, rows), :],
                                         sem.at[buf])
        return make

    def fetch(list_ref, buf):
        loc_scr[buf] = jnp.zeros((n_loc, d), F32)
        _start_chunks(list_ref, maker(buf))

    @pl.when(i == 0)
    def _():
        fetch(dst_ref, cur)

    @pl.when(i + 1 < pl.num_programs(0))
    def _():
        fetch(nxt_ref, 1 - cur)

    make = maker(cur)
    n = dst_ref[0, 0, dst_ref.shape[2] - 1]

    slots = _local_slots(idx_ref, rank_ref, loff_ref)
    w = wts_ref[...]
    col = lax.broadcasted_iota(I32, (tb, n_loc), 1).astype(F32)
    wmat = jnp.zeros((tb, n_loc), F32)
    for k in range(TOP_K):
        wmat = jnp.where(col == slots[k], w[:, k:k + 1], wmat)
    wmat = wmat.astype(BF16)

    _wait_chunks(n, make, n_loc // CHUNK)
    y = _dot(wmat, loc_scr[cur].astype(BF16))
    m = mod_ref[0]
    o_ref[...] = _ln(alpha * x1_ref[...] + m[:, 5 * d:6 * d] * y) * lg_ref[...] + lb_ref[...]


def _combine(ys, idx, rank, wts, loff_vec, chunk_dst, x1, mod3, ln_g, ln_b, mod_row, *, tb, alpha):
    t, d = x1.shape
    n_loc = tb * TOP_K + N_EXPERTS * CHUNK
    tile = lambda i: (i, 0)
    const2 = lambda i: (0, 0)
    nblk = t // tb
    list_shape = (1, 1, chunk_dst.shape[2])
    return pl.pallas_call(
        functools.partial(_combine_kernel, d=d, alpha=alpha),
        grid=(nblk,),
        in_specs=[pl.BlockSpec(list_shape, lambda i: (i, 0, 0), memory_space=pltpu.SMEM),
                  pl.BlockSpec(list_shape, lambda i: (jnp.minimum(i + 1, nblk - 1), 0, 0), memory_space=pltpu.SMEM),
                  pl.BlockSpec((tb, LANES), tile), pl.BlockSpec((tb, LANES), tile), pl.BlockSpec((tb, LANES), tile),
                  pl.BlockSpec((1, 1, LANES), lambda i: (i, 0, 0)),
                  pl.BlockSpec((tb, d), tile),
                  pl.BlockSpec((1, 1, 6 * d), lambda i: (mod_row(i, tb), 0, 0)),
                  pl.BlockSpec((1, d), const2), pl.BlockSpec((1, d), const2),
                  pl.BlockSpec(memory_space=pl.ANY)],
        out_specs=pl.BlockSpec((tb, d), tile),
        out_shape=jax.ShapeDtypeStruct((t, d), F32),
        scratch_shapes=[pltpu.VMEM((2, n_loc, d), F32), pltpu.SemaphoreType.DMA((2,))],
        compiler_params=_params("arbitrary"),
        name="combine",
    )(chunk_dst, chunk_dst, idx, rank, wts, loff_vec, x1, mod3, ln_g.reshape(1, d), ln_b.reshape(1, d), ys)


def _moe(tok, x1, mod3, mod_row, router_w, router_b, w1, b1, w2, b2, ln_g, ln_b, *, alpha, layer):
    t = tok.shape[0]
    tile, tb = MOE_TILE, MOE_BLOCK
    nblk = t // tb
    idx, wts, rank, counts = _router(tok, router_w, router_b, tb=tb, tm=math.gcd(2 * tb, t))
    n = counts[:, 0, :N_EXPERTS].astype(I32)
    npad = (n + CHUNK - 1) // CHUNK * CHUNK
    loff = jnp.cumsum(npad, axis=1) - npad
    padded = (jnp.sum(npad, axis=0) + tile - 1) // tile * tile
    pad_end = jnp.cumsum(padded)
    goff = (pad_end - padded)[None, :] + jnp.cumsum(npad, axis=0) - npad
    loff_vec = jnp.zeros((nblk, 1, LANES), I32).at[:, 0, :N_EXPERTS].set(loff)
    max_chunks = (tb * TOP_K + N_EXPERTS * CHUNK) // CHUNK
    ch_end = jnp.cumsum(npad, axis=1) // CHUNK
    f = jnp.arange(max_chunks, dtype=I32)
    owner = jnp.minimum(jnp.sum(ch_end[:, None, :] <= f[None, :, None], axis=2), N_EXPERTS - 1)
    is_owner = owner[:, :, None] == jnp.arange(N_EXPERTS, dtype=I32)[None, None, :]
    dst = jnp.sum(jnp.where(is_owner, (goff - loff)[:, None, :], 0), axis=2) + f[None, :] * CHUNK
    chunk_dst = jnp.zeros((nblk, 1, 2 * LANES), I32).at[:, 0, :max_chunks].set(dst).at[:, 0, -1].set(ch_end[:, -1])
    flat = lambda a: a.reshape(-1).astype(I32)
    n_tiles = -(-(t * TOP_K + (CHUNK - 1) * N_EXPERTS * nblk) // tile) + N_EXPERTS
    n_used = (pad_end[-1] // tile).astype(I32).reshape(1)
    tile_start = jnp.minimum(jnp.arange(n_tiles, dtype=I32), n_used[0] - 1) * tile
    tile_expert = jnp.minimum(jnp.sum(pad_end[None, :] <= tile_start[:, None], axis=1), N_EXPERTS - 1).astype(I32)
    xs = _dispatch(tok, idx, rank, loff_vec, chunk_dst, flat(pad_end), flat(padded), tb=tb, tile=tile,
                   n_slots=n_tiles * tile)
    ys = _experts(xs, w1, b1, w2, b2, tile_expert, n_used, tile=tile, layer=layer)
    return _combine(ys, idx, rank, wts, loff_vec, chunk_dst, x1, mod3, ln_g, ln_b, mod_row, tb=tb, alpha=alpha)


def kernel(x, c, ctx, c_ctx, w_ada, b_ada, w_in, conv_dw_w, conv_dw_b, conv_ln_g, conv_ln_b, lam_q1, lam_k1, lam_q2,
           lam_k2, diff_subln_g, nat_rpb, w_branch, w_out, ln1_g, ln1_b, router_w, router_b, exp_w1, exp_b1, exp_w2,
           exp_b2, ln2_g, ln2_b):
    batch, seq, d = x.shape
    n_ctx = ctx.shape[1]
    depth = w_ada.shape[0]
    half = d // 2
    nlat, nctx = batch * seq, batch * n_ctx
    alpha = (2 * depth) ** 0.25
    assert seq % (GRID_W * NAT_QROWS) == 0 and seq // GRID_W >= NAT_SPAN_ROWS and seq % n_ctx == 0

    x_parts = (x.reshape(nlat, d), ctx.reshape(nctx, d))
    w_in_bf, w_branch_bf, w_out_bf = w_in.astype(BF16), w_branch.astype(BF16), w_out.astype(BF16)
    mod_rows = -(-(batch + 1) // SUBLANES) * SUBLANES
    cc = jnp.zeros((mod_rows, d), F32).at[:batch].set(c).at[batch].set(c_ctx)
    mod3 = _ada(cc, w_ada, b_ada).reshape(depth * mod_rows, 1, 6 * d)
    cos, sin = _rope_tables(seq, half // DIFF_D)

    for i in range(depth):
        with_ctx = i < depth - 1
        lam_init = 0.8 - 0.6 * math.exp(-0.3 * i)
        n_tok = nlat + nctx if with_ctx else nlat

        def mod_row(t, tm, i=i):
            return i * mod_rows + jnp.where(t * tm < nlat, (t * tm) // seq, batch)

        u = _inproj(x_parts, mod3, w_in_bf, mod_row, n_rows=nlat + nctx, tm=math.gcd(1024, nctx), tn=1792, layer=i)

        conv_p = (conv_dw_w[i], conv_dw_b[i], conv_ln_g[i], conv_ln_b[i])
        ya = _conv(u, *conv_p, seq=seq, nseq=batch, row_block0=0)
        q_rot, k_rot = _rope(u, cos, sin, nlat=nlat, seq=seq, tm=1024, qcol=2, kcol=3)
        lam_vecs = jnp.stack([lam_q1[i], lam_k1[i], lam_q2[i], lam_k2[i]]).astype(F32)
        yb = _diff_attn(q_rot, k_rot, u, lam_vecs, diff_subln_g[i], batch=batch, seq=seq, n_ctx=n_ctx, tq=512,
                        lam_init=lam_init, kcol=3, vcol=4)
        bias = _nat_bias_tables(nat_rpb[i], seq // GRID_W)
        yc = _nat(u, bias, batch=batch, seq=seq, n_ctx=n_ctx, qcol=5, kcol=6, vcol=7)
        y_lat = (ya, yb, yc)
        y_ctx = y_lat
        if with_ctx:
            ya_c = _conv(u, *conv_p, seq=n_ctx, nseq=batch, row_block0=nlat // n_ctx)
            yb_c, yc_c = _ctx_attn(u, lam_vecs, diff_subln_g[i], batch=batch, seq=seq, n_ctx=n_ctx, width=half,
                                   lam_init=lam_init, cols=(2, 3, 4, 5, 6, 7))
            y_ctx = (ya_c, yb_c, yc_c)

        x1, tok = _merge(y_lat, y_ctx, u, x_parts, mod3, w_branch_bf, w_out_bf, ln1_g[i], ln1_b[i], mod_row,
                         n_tok=n_tok, tm=math.gcd(512, nctx), gcol0=4, alpha=alpha, layer=i)
        xa = _moe(tok, x1, mod3, mod_row, router_w[i], router_b[i], exp_w1, exp_b1, exp_w2, exp_b2, ln2_g[i], ln2_b[i],
                  alpha=alpha, layer=i)
        x_parts = (xa, xa)
    return xa[:nlat].reshape(batch, seq, d)
```
